```python
import jax
import jax.numpy as jnp
from jax import lax
import numpy as np

D_MODEL = 1024
BATCH = 2
SEQ = 8192
DEPTH = 2

D_MIX = D_MODEL
RW_HEADS = 6
RW_HEAD_DIM = 64
RW_WIDTH = RW_HEADS * RW_HEAD_DIM
RW_DECAY_LORA = 64
RW_ICLR_LORA = 64
RW_VRES_LORA = 32
RW_SHIFT_DIM = 3 * RW_WIDTH + RW_DECAY_LORA + RW_ICLR_LORA
RW_GN_EPS = 64e-5
HG_HEADS = 6
HG_EXPAND = 64
HG_HEAD_DIM = 64
HG_KEY = HG_HEADS * HG_EXPAND
HG_WIDTH = HG_HEADS * HG_HEAD_DIM
HG_CHUNK = 64
LRU_WIDTH = D_MIX - RW_WIDTH - HG_WIDTH
LRU_BLOCKS = 4
LRU_BLOCK_DIM = LRU_WIDTH // LRU_BLOCKS
CONV_WIDTH = 4
LRU_C = 8.0
NORM_EPS = 1e-6
IN_SEGMENTS = (RW_SHIFT_DIM, RW_WIDTH,
               HG_KEY, HG_KEY, HG_WIDTH, HG_WIDTH,
               LRU_WIDTH, LRU_WIDTH)
N_IN = sum(IN_SEGMENTS)

kernel_name = "hymba_rwkv7_hgrn2_rglru_trunk"


def _rmsnorm(x, w, eps=NORM_EPS):
    xf = x.astype(jnp.float32)
    y = xf * lax.rsqrt(jnp.mean(xf * xf, axis=-1, keepdims=True) + eps)
    return y * w.astype(jnp.float32)


def _token_shift(u):
    return jnp.pad(u, ((0, 0), (1, 0), (0, 0)))[:, :-1, :]


def _rwkv7_recurrence(r, w, k, v, a_in, b_in):
    B, T, H, N = r.shape

    def step(S, inp):
        r_t, w_t, k_t, v_t, a_t, b_t = inp
        sa = jnp.einsum('bhvk,bhk->bhv', S, a_t)
        S = (S * w_t[:, :, None, :] + sa[..., None] * b_t[:, :, None, :]
             + v_t[..., None] * k_t[:, :, None, :])
        return S, jnp.einsum('bhvk,bhk->bhv', S, r_t)

    xs = (jnp.moveaxis(r, 1, 0), jnp.moveaxis(w, 1, 0), jnp.moveaxis(k, 1, 0),
          jnp.moveaxis(v, 1, 0), jnp.moveaxis(a_in, 1, 0), jnp.moveaxis(b_in, 1, 0))
    S0 = jnp.zeros((B, H, N, N), jnp.float32)
    _, ys = lax.scan(step, S0, xs)
    return jnp.moveaxis(ys, 0, 1)


def _rwkv7_mixer(u_shift, z, v_first, mu, w0, w_up, a0, a_up, k_k, k_a, r_k, ln_w, ln_b, vres):
    B, T, _ = u_shift.shape
    u = u_shift.astype(jnp.float32)
    u = u + mu * (_token_shift(u) - u)
    r, k, v, wl, al = jnp.split(
        u, [RW_WIDTH, 2 * RW_WIDTH, 3 * RW_WIDTH, 3 * RW_WIDTH + RW_DECAY_LORA], axis=-1)
    w_log = -jax.nn.softplus(-(w0 + jnp.tanh(wl) @ w_up)) - 0.5
    w = jnp.exp(-jnp.exp(w_log))
    a = jax.nn.sigmoid(a0 + al @ a_up)
    if vres is None:
        v_first = v
    else:
        v0, v_dn, v_up = vres
        v = v + (v_first - v) * jax.nn.sigmoid(v0 + (v @ v_dn) @ v_up)

    def hd(t):
        return t.reshape(B, T, RW_HEADS, RW_HEAD_DIM)

    kk = hd(k * k_k)
    kk = kk / jnp.maximum(jnp.sqrt(jnp.sum(kk * kk, axis=-1, keepdims=True)), 1e-12)
    k = k * (1.0 + (a - 1.0) * k_a)
    y = _rwkv7_recurrence(hd(r), hd(w), hd(k), hd(v), -kk, kk * hd(a))
    mean = jnp.mean(y, axis=-1, keepdims=True)
    var = jnp.mean(jnp.square(y - mean), axis=-1, keepdims=True)
    y = ((y - mean) * lax.rsqrt(var + RW_GN_EPS)).reshape(B, T, RW_WIDTH) * ln_w + ln_b
    bonus = jnp.sum(hd(r) * hd(k) * r_k, axis=-1, keepdims=True) * hd(v)
    out = (y + bonus.reshape(B, T, RW_WIDTH)) * jax.nn.silu(z.astype(jnp.float32))
    return out, v_first


def _hgrn2_mixer(q, f_raw, i, z, lb, norm_w):
    B, T, _ = q.shape
    n_chunks = T // HG_CHUNK
    f = lb + (1.0 - lb) * jax.nn.sigmoid(f_raw.astype(jnp.float32))
    g = jnp.log(f)
    k = 1.0 - f

    def to_chunks(t, d):
        return t.astype(jnp.float32).reshape(B, n_chunks, HG_CHUNK, HG_HEADS, d).transpose(1, 0, 3, 2, 4)

    causal = jnp.tril(jnp.ones((HG_CHUNK, HG_CHUNK), dtype=bool))[:, :, None]

    def chunk_step(S, inp):
        q_c, k_c, v_c, g_c = inp
        b = jnp.cumsum(g_c, axis=2)
        o_inter = jnp.einsum('bhtk,bhkv->bhtv', q_c * jnp.exp(b), S)
        diff = b[:, :, :, None, :] - b[:, :, None, :, :]
        decay = jnp.where(causal, jnp.exp(jnp.where(causal, diff, 0.0)), 0.0)
        scores = jnp.einsum('bhtk,bhsk,bhtsk->bhts', q_c, k_c, decay)
        o = o_inter + jnp.einsum('bhts,bhsv->bhtv', scores, v_c)
        b_last = b[:, :, -1:, :]
        S = (S * jnp.exp(b_last)[:, :, 0, :, None]
             + jnp.einsum('bhsk,bhsv->bhkv', k_c * jnp.exp(b_last - b), v_c))
        return S, o

    S0 = jnp.zeros((B, HG_HEADS, HG_EXPAND, HG_HEAD_DIM), jnp.float32)
    xs = (to_chunks(q, HG_EXPAND), to_chunks(k, HG_EXPAND),
          to_chunks(i, HG_HEAD_DIM), to_chunks(g, HG_EXPAND))
    _, o = lax.scan(chunk_step, S0, xs)
    o = o.transpose(1, 0, 3, 2, 4).reshape(B, T, HG_HEADS, HG_HEAD_DIM)
    o = o * lax.rsqrt(jnp.mean(o * o, axis=-1, keepdims=True) + NORM_EPS)
    return o.reshape(B, T, HG_WIDTH) * norm_w * jax.nn.silu(z.astype(jnp.float32))


def _lin_rec_combine(left, right):
    a_l, b_l = left
    a_r, b_r = right
    return a_l * a_r, a_r * b_l + b_r


def _rglru_mixer(xb, z, conv_w, conv_b, wa, ba, wx, bx, lam):
    B, T, W = xb.shape
    xf = xb.astype(jnp.float32)
    y = lax.conv_general_dilated(
        xf, conv_w.astype(jnp.float32)[:, None, :], window_strides=(1,),
        padding=((CONV_WIDTH - 1, 0),), dimension_numbers=('NWC', 'WIO', 'NWC'),
        feature_group_count=W) + conv_b
    yg = y.reshape(B, T, LRU_BLOCKS, LRU_BLOCK_DIM)
    r = jax.nn.sigmoid(jnp.einsum('btgi,gij->btgj', yg, wa).reshape(B, T, W) + ba)
    ig = jax.nn.sigmoid(jnp.einsum('btgi,gij->btgj', yg, wx).reshape(B, T, W) + bx)
    log_a = -LRU_C * r * jax.nn.softplus(-lam)
    a = jnp.exp(log_a)
    u = jnp.sqrt(-jnp.expm1(2.0 * log_a)) * (ig * y)
    _, h = lax.associative_scan(_lin_rec_combine, (a, u), axis=1)
    return h * jax.nn.silu(z.astype(jnp.float32))


def setup_inputs(seed: int = 0) -> dict:
    key = jax.random.key(seed)
    ks = jax.random.split(key, 28)
    f32 = jnp.float32
    L = DEPTH

    def nrm(k, shape, scale):
        return scale * jax.random.normal(k, shape, f32)

    a_base = jax.random.uniform(ks[24], (L, LRU_WIDTH), f32, 0.9, 0.999) ** (1.0 / LRU_C)
    return {
        "x": nrm(ks[0], (BATCH, SEQ, D_MODEL), 1.0),
        "norm_w": 1.0 + nrm(ks[1], (L, D_MODEL), 0.02),
        "w_in": nrm(ks[2], (L, D_MODEL, N_IN), D_MODEL ** -0.5),
        "rw_mu": jax.random.uniform(ks[3], (L, RW_SHIFT_DIM), f32),
        "rw_w0": jax.random.uniform(ks[4], (L, RW_WIDTH), f32, -5.0, 1.0),
        "rw_w_up": nrm(ks[5], (L, RW_DECAY_LORA, RW_WIDTH), 0.1 * RW_DECAY_LORA ** -0.5),
        "rw_a0": nrm(ks[6], (L, RW_WIDTH), 0.5),
        "rw_a_up": nrm(ks[7], (L, RW_ICLR_LORA, RW_WIDTH), 0.1 * RW_ICLR_LORA ** -0.5),
        "rw_v0": 1.0 + nrm(ks[8], (L - 1, RW_WIDTH), 0.1),
        "rw_v_dn": nrm(ks[9], (L - 1, RW_WIDTH, RW_VRES_LORA), RW_WIDTH ** -0.5),
        "rw_v_up": nrm(ks[10], (L - 1, RW_VRES_LORA, RW_WIDTH), 0.1 * RW_VRES_LORA ** -0.5),
        "rw_k_k": 0.85 + nrm(ks[11], (L, RW_WIDTH), 0.02),
        "rw_k_a": 1.0 + nrm(ks[12], (L, RW_WIDTH), 0.02),
        "rw_r_k": nrm(ks[13], (L, RW_HEADS, RW_HEAD_DIM), 0.1),
        "rw_ln_w": 1.0 + nrm(ks[14], (L, RW_WIDTH), 0.02),
        "rw_ln_b": nrm(ks[15], (L, RW_WIDTH), 0.02),
        "hg_lb_raw": nrm(ks[16], (L, HG_KEY), 1.0),
        "hg_norm_w": 1.0 + nrm(ks[17], (L, HG_WIDTH), 0.02),
        "lru_conv_w": nrm(ks[18], (L, CONV_WIDTH, LRU_WIDTH), CONV_WIDTH ** -0.5),
        "lru_conv_b": nrm(ks[19], (L, LRU_WIDTH), 0.02),
        "lru_wa": nrm(ks[20], (L, LRU_BLOCKS, LRU_BLOCK_DIM, LRU_BLOCK_DIM), LRU_BLOCK_DIM ** -0.5),
        "lru_ba": nrm(ks[21], (L, LRU_WIDTH), 0.02),
        "lru_wx": nrm(ks[22], (L, LRU_BLOCKS, LRU_BLOCK_DIM, LRU_BLOCK_DIM), LRU_BLOCK_DIM ** -0.5),
        "lru_bx": nrm(ks[23], (L, LRU_WIDTH), 0.02),
        "lru_lambda": jnp.log(a_base) - jnp.log1p(-a_base),
        "w_out": nrm(ks[25], (L, D_MIX, D_MODEL), D_MIX ** -0.5),
        "final_norm_w": 1.0 + nrm(ks[26], (D_MODEL,), 0.02),
    }


def reference(x, norm_w, w_in, rw_mu, rw_w0, rw_w_up, rw_a0, rw_a_up, rw_v0, rw_v_dn, rw_v_up,
              rw_k_k, rw_k_a, rw_r_k, rw_ln_w, rw_ln_b, hg_lb_raw, hg_norm_w,
              lru_conv_w, lru_conv_b, lru_wa, lru_ba, lru_wx, lru_bx, lru_lambda,
              w_out, final_norm_w):
    splits = np.cumsum(IN_SEGMENTS)[:-1].tolist()
    lb_w = jax.nn.softmax(hg_lb_raw.astype(jnp.float32), axis=0)
    lower_bounds = jnp.cumsum(lb_w, axis=0) - lb_w[0]
    v_first = None
    for l in range(DEPTH):
        h = _rmsnorm(x, norm_w[l]).astype(x.dtype)
        u = h @ w_in[l]
        rw_u, rw_z, hg_q, hg_f, hg_i, hg_z, lru_x, lru_z = jnp.split(u, splits, axis=-1)
        vres = None if l == 0 else (rw_v0[l - 1], rw_v_dn[l - 1], rw_v_up[l - 1])
        rw_out, v_first = _rwkv7_mixer(rw_u, rw_z, v_first, rw_mu[l], rw_w0[l], rw_w_up[l],
                                       rw_a0[l], rw_a_up[l], rw_k_k[l], rw_k_a[l], rw_r_k[l],
                                       rw_ln_w[l], rw_ln_b[l], vres)
        hg_out = _hgrn2_mixer(hg_q, hg_f, hg_i, hg_z, lower_bounds[l], hg_norm_w[l])
        lru_out = _rglru_mixer(lru_x, lru_z, lru_conv_w[l], lru_conv_b[l], lru_wa[l], lru_ba[l],
                               lru_wx[l], lru_bx[l], lru_lambda[l])
        mix = jnp.concatenate([rw_out, hg_out, lru_out], axis=-1).astype(x.dtype)
        x = x + mix @ w_out[l]
    return _rmsnorm(x, final_norm_w).astype(x.dtype)
```

```python
import functools

import numpy as np
import jax
import jax.numpy as jnp
from jax import lax
from jax.experimental import pallas as pl
from jax.experimental.pallas import tpu as pltpu

F32 = jnp.float32
BF16 = jnp.bfloat16

NORM_EPS = 1e-6
RW_GN_EPS = 64e-5
LRU_C = 8.0
HEAD = 64
LANES = 128
CHUNK = 64
HG_BLOCK = 16
CONV_WIDTH = 4
LOG2_HEAD = HEAD.bit_length() - 1
LOG2_CHUNK = CHUNK.bit_length() - 1
V7X_VMEM_BYTES = 64 * 1024 * 1024


def _dot(a, b):
    return jnp.dot(a, b, preferred_element_type=F32)


def _dot_nt(a, b):
    return lax.dot_general(a, b, (((1,), (1,)), ((), ())), preferred_element_type=F32)


def _dot_tn(a, b):
    return lax.dot_general(a, b, (((0,), (0,)), ((), ())), preferred_element_type=F32)


def _mm(a, b):
    return _dot(a.astype(BF16), b.astype(BF16))


def _split2(x):
    hi = x.astype(BF16)
    lo = (x - hi.astype(F32)).astype(BF16)
    return hi, lo


def _split3(x):
    hi = x.astype(BF16)
    r1 = x - hi.astype(F32)
    mid = r1.astype(BF16)
    lo = (r1 - mid.astype(F32)).astype(BF16)
    return hi, mid, lo


def _mm_left_exact(l_bf16, x):
    hi, mid, lo = _split3(x)
    return _dot(l_bf16, hi) + _dot(l_bf16, mid) + _dot(l_bf16, lo)


def _mm_right_exact2(x, r_bf16):
    hi, lo = _split2(x)
    return _dot(hi, r_bf16) + _dot(lo, r_bf16)


def _mm_hp(a, b):
    ah, al = _split2(a)
    bh, bl = _split2(b)
    return _dot(ah, bh) + _dot(ah, bl) + _dot(al, bh)


def _sigmoid(x):
    return 1.0 / (1.0 + jnp.exp(-x))


def _silu(x):
    return x * _sigmoid(x)


def _softplus(x):
    return jnp.maximum(x, 0.0) + jnp.log1p(jnp.exp(-jnp.abs(x)))


def _iota(shape, dim):
    return lax.broadcasted_iota(jnp.int32, shape, dim)


def _inproj_kernel(x_ref, nw_ref, w_ref, o_ref):
    x = x_ref[...]
    ms = jnp.mean(x * x, axis=-1, keepdims=True)
    h = x * lax.rsqrt(ms + NORM_EPS) * nw_ref[...]
    o_ref[...] = _dot(h.astype(BF16), w_ref[...])


def _outproj_kernel(x_ref, m_ref, w_ref, o_ref):
    o_ref[...] = x_ref[...] + _dot(m_ref[...], w_ref[...])


def _outproj_final_kernel(x_ref, m_ref, w_ref, fw_ref, o_ref):
    x = x_ref[...] + _dot(m_ref[...], w_ref[...])
    ms = jnp.mean(x * x, axis=-1, keepdims=True)
    o_ref[...] = x * lax.rsqrt(ms + NORM_EPS) * fw_ref[...]


def _row_block(n_rows, target):
    blk = min(target, n_rows)
    while n_rows % blk:
        blk //= 2
    return blk


def _vmem_limit(block_bytes):
    return int(min(V7X_VMEM_BYTES * 7 // 8, 2 * block_bytes + 16 * 1024 * 1024))


def _inproj(x2, norm_w, w_bf16):
    n, d = x2.shape
    n_in = w_bf16.shape[1]
    tm = _row_block(n, 256)
    block_bytes = tm * d * 4 + d * n_in * 2 + tm * n_in * 4
    return pl.pallas_call(
        _inproj_kernel,
        out_shape=jax.ShapeDtypeStruct((n, n_in), F32),
        grid=(n // tm,),
        in_specs=[pl.BlockSpec((tm, d), lambda i: (i, 0)),
                  pl.BlockSpec((1, d), lambda i: (0, 0)),
                  pl.BlockSpec((d, n_in), lambda i: (0, 0))],
        out_specs=pl.BlockSpec((tm, n_in), lambda i: (i, 0)),
        compiler_params=pltpu.CompilerParams(
            dimension_semantics=("arbitrary",), vmem_limit_bytes=_vmem_limit(block_bytes)),
        name="inproj",
    )(x2, norm_w, w_bf16)


def _outproj(x2, mix, w_bf16, final_w=None):
    n, d = x2.shape
    dm = mix.shape[1]
    tm = _row_block(n, 512)
    block_bytes = 2 * tm * d * 4 + tm * dm * 2 + dm * d * 2
    in_specs = [pl.BlockSpec((tm, d), lambda i: (i, 0)),
                pl.BlockSpec((tm, dm), lambda i: (i, 0)),
                pl.BlockSpec((dm, d), lambda i: (0, 0))]
    args = [x2, mix, w_bf16]
    body = _outproj_kernel
    if final_w is not None:
        in_specs.append(pl.BlockSpec((1, d), lambda i: (0, 0)))
        args.append(final_w)
        body = _outproj_final_kernel
    return pl.pallas_call(
        body,
        out_shape=jax.ShapeDtypeStruct((n, d), F32),
        grid=(n // tm,),
        in_specs=in_specs,
        out_specs=pl.BlockSpec((tm, d), lambda i: (i, 0)),
        compiler_params=pltpu.CompilerParams(
            dimension_semantics=("arbitrary",), vmem_limit_bytes=_vmem_limit(block_bytes)),
        name="outproj_final" if final_w is not None else "outproj",
    )(*args)


def _stack_heads(x, lane_head0):
    return jnp.concatenate([jnp.where(lane_head0, x, 0.0), jnp.where(lane_head0, 0.0, x)], axis=0)


def _mix_kernel(*refs, layer, has_vres, n_chunks, rw_w, hg_w, lru_w):
    it = iter(refs)
    u_ref = next(it)
    vfirst_in_ref = next(it) if has_vres else None
    mu_ref, w0_ref, wup_ref, a0_ref, aup_ref = (next(it) for _ in range(5))
    if has_vres:
        v0_ref, vdn_ref, vup_ref = (next(it) for _ in range(3))
    kk_ref, ka_ref, rk_ref, lnw_ref, lnb_ref = (next(it) for _ in range(5))
    lbraw_ref, hgnw_ref = (next(it) for _ in range(2))
    convw_ref, convb_ref, wa_ref, ba_ref, wx_ref, bx_ref, lam_ref = (next(it) for _ in range(7))
    tri_ref, bdtri_ref, bdup_ref, seg_ref = (next(it) for _ in range(4))
    out_ref = next(it)
    vfirst_out_ref = None if has_vres else next(it)
    rwprev_ref, rwstate_ref, hgstate_ref, lrux_ref, lruh_ref = (next(it) for _ in range(5))

    n_pairs = rw_w // LANES
    rw_shift = 3 * rw_w + LANES
    o_rwz = rw_shift
    o_hgq = o_rwz + rw_w
    o_hgf = o_hgq + hg_w
    o_hgi = o_hgf + hg_w
    o_hgz = o_hgi + hg_w
    o_lrx = o_hgz + hg_w
    o_lrz = o_lrx + lru_w

    @pl.when(pl.program_id(1) == 0)
    def _reset():
        rwprev_ref[...] = jnp.zeros_like(rwprev_ref)
        rwstate_ref[...] = jnp.zeros_like(rwstate_ref)
        hgstate_ref[...] = jnp.zeros_like(hgstate_ref)
        lrux_ref[...] = jnp.zeros_like(lrux_ref)
        lruh_ref[...] = jnp.zeros_like(lruh_ref)

    row_c = _iota((CHUNK, 1), 0)
    lane_head0 = _iota((1, LANES), 1) < HEAD
    si = _iota((2 * CHUNK, 2 * CHUNK), 0)
    sj = _iota((2 * CHUNK, 2 * CHUNK), 1)
    same_head = (si >> LOG2_CHUNK) == (sj >> LOG2_CHUNK)
    ti = si & (CHUNK - 1)
    tj = sj & (CHUNK - 1)
    strict_lower = same_head & (tj < ti)
    incl_lower = same_head & (tj <= ti)
    eye = si == sj
    bd_state = (_iota((LANES, LANES), 0) >> LOG2_HEAD) == (_iota((LANES, LANES), 1) >> LOG2_HEAD)

    seg = seg_ref[...]

    def segsum(x):
        return _mm_right_exact2(x, seg)

    lbraw = lbraw_ref[...]
    lbe = jnp.exp(lbraw - jnp.max(lbraw, axis=0, keepdims=True))
    lbw = lbe / jnp.sum(lbe, axis=0, keepdims=True)
    lb = lbw[0:1, :]
    for j in range(1, layer + 1):
        lb = lb + lbw[j:j + 1, :]
    lb = lb - lbw[0:1, :]

    def chunk_body(c, carry):
        r0 = pl.multiple_of(c * CHUNK, CHUNK)
        rows = pl.ds(r0, CHUNK)

        ru = u_ref[rows, 0:rw_shift]
        sh = jnp.where(row_c == 0, rwprev_ref[...], pltpu.roll(ru, 1, axis=0))
        rwprev_ref[...] = ru[CHUNK - 1:CHUNK, :]
        ul = ru + mu_ref[...] * (sh - ru)
        r = ul[:, 0:rw_w]
        k = ul[:, rw_w:2 * rw_w]
        v = ul[:, 2 * rw_w:3 * rw_w]
        codes = ul[:, 3 * rw_w:rw_shift]
        w_log = -_softplus(-(w0_ref[...] + _mm(jnp.tanh(codes), wup_ref[...]))) - 0.5
        nlw = jnp.exp(w_log)
        a = _sigmoid(a0_ref[...] + _mm(codes, aup_ref[...]))
        if has_vres:
            vf = vfirst_in_ref[rows, :]
            gate = _sigmoid(v0_ref[...] + _mm(_mm(v, vdn_ref[...]), vup_ref[...]))
            v = v + (vf - v) * gate
        else:
            vfirst_out_ref[rows, :] = v
        kk = k * kk_ref[...]
        kk = kk / jnp.maximum(jnp.sqrt(segsum(kk * kk)), 1e-12)
        k2 = k * (1.0 + (a - 1.0) * ka_ref[...])
        a_in = -kk
        b_in = kk * a
        bonus = segsum(r * k2 * rk_ref[...]) * v

        cs = _mm_left_exact(tri_ref[...], nlw)
        e_pos = jnp.exp(-cs)
        e_neg = jnp.exp(cs)
        e_prev = jnp.exp(nlw - cs)
        a_t = a_in * e_prev
        r_t = r * e_pos
        b_t = b_in * e_neg
        k_t = k2 * e_neg

        y_pairs = []
        for p in range(n_pairs):
            ls = slice(p * LANES, (p + 1) * LANES)
            lhs = jnp.concatenate([_stack_heads(a_t[:, ls], lane_head0),
                                   _stack_heads(r_t[:, ls], lane_head0)], axis=0).astype(BF16)
            rhs = jnp.concatenate([_stack_heads(b_t[:, ls], lane_head0),
                                   _stack_heads(k_t[:, ls], lane_head0)], axis=0).astype(BF16)
            sc = _dot_nt(lhs, rhs)
            c2 = 2 * CHUNK
            a_ab = jnp.where(strict_lower, sc[0:c2, 0:c2], 0.0)
            a_ak = jnp.where(strict_lower, sc[0:c2, c2:2 * c2], 0.0)
            l_rb = jnp.where(incl_lower, sc[c2:2 * c2, 0:c2], 0.0)
            l_rk = jnp.where(incl_lower, sc[c2:2 * c2, c2:2 * c2], 0.0)
            pw = a_ab
            tinv = jnp.where(eye, 1.0, 0.0) + pw
            for _ in range(5):
                pw = _mm_hp(pw, pw)
                tinv = tinv + _mm_hp(tinv, pw)
            st = rwstate_ref[p]
            am = _dot_nt(lhs, st.astype(BF16))
            vs = _stack_heads(v[:, ls], lane_head0)
            x = am[0:c2] + _mm(a_ak, vs)
            uu = _mm(tinv, x)
            uv = jnp.concatenate([uu, vs], axis=0).astype(BF16)
            yy = am[c2:2 * c2] + _dot(jnp.concatenate([l_rb, l_rk], axis=1).astype(BF16), uv)
            y_pairs.append(yy[0:CHUNK] + yy[CHUNK:c2])
            rwstate_ref[p] = (st + _dot_tn(uv, rhs)) * e_pos[CHUNK - 1:CHUNK, ls]
        y = jnp.concatenate(y_pairs, axis=1)
        mean = segsum(y) * (1.0 / HEAD)
        yc = y - mean
        var = segsum(yc * yc) * (1.0 / HEAD)
        yn = yc * lax.rsqrt(var + RW_GN_EPS) * lnw_ref[...] + lnb_ref[...]
        rw_out = (yn + bonus) * _silu(u_ref[rows, o_rwz:o_rwz + rw_w])
        out_ref[rows, 0:rw_w] = rw_out.astype(out_ref.dtype)

        q = u_ref[rows, o_hgq:o_hgq + hg_w]
        f = lb + (1.0 - lb) * _sigmoid(u_ref[rows, o_hgf:o_hgf + hg_w])
        g = jnp.log(f)
        kx = 1.0 - f
        iv = u_ref[rows, o_hgi:o_hgi + hg_w]
        bcum = _mm_left_exact(bdtri_ref[...], g)
        brev = _mm_left_exact(bdup_ref[...], g)
        pos = row_c & (HG_BLOCK - 1)
        acc = jnp.zeros((CHUNK, hg_w), F32)
        for d in range(HG_BLOCK):
            valid = pos >= d
            if d == 0:
                wgt = q * kx
                vsd = iv
            else:
                dd = jnp.where(valid, bcum - pltpu.roll(bcum, d, axis=0), 0.0)
                wgt = jnp.where(valid, q * jnp.exp(dd) * pltpu.roll(kx, d, axis=0), 0.0)
                vsd = pltpu.roll(iv, d, axis=0)
            acc = acc + _dot(wgt.astype(BF16), seg) * vsd
        qh = q * jnp.exp(bcum)
        kh = kx * jnp.exp(brev)
        o_pairs = []
        for p in range(hg_w // LANES):
            ls = slice(p * LANES, (p + 1) * LANES)
            st = hgstate_ref[p]
            blocks = []
            for j in range(CHUNK // HG_BLOCK):
                rs = slice(j * HG_BLOCK, (j + 1) * HG_BLOCK)
                blocks.append(_dot_nt(qh[rs, ls].astype(BF16), st.astype(BF16)))
                upd = _dot_tn(iv[rs, ls].astype(BF16), kh[rs, ls].astype(BF16))
                last = (j + 1) * HG_BLOCK - 1
                st = st * jnp.exp(bcum[last:last + 1, ls]) + jnp.where(bd_state, upd, 0.0)
            hgstate_ref[p] = st
            o_pairs.append(jnp.concatenate(blocks, axis=0))
        o = acc + jnp.concatenate(o_pairs, axis=1)
        ms = segsum(o * o) * (1.0 / HEAD)
        hg_out = o * lax.rsqrt(ms + NORM_EPS) * hgnw_ref[...] * _silu(u_ref[rows, o_hgz:o_hgz + hg_w])
        out_ref[rows, rw_w:rw_w + hg_w] = hg_out.astype(out_ref.dtype)

        xb = u_ref[rows, o_lrx:o_lrx + lru_w]
        tail = lrux_ref[...]
        row8 = _iota((8, 1), 0)
        yv = convb_ref[...] + convw_ref[CONV_WIDTH - 1:CONV_WIDTH, :] * xb
        for d in range(1, CONV_WIDTH):
            rolled = pltpu.roll(xb, d, axis=0)
            head8 = jnp.where(row8 < d, pltpu.roll(tail, d, axis=0), rolled[0:8])
            xs = jnp.concatenate([head8, rolled[8:]], axis=0)
            yv = yv + convw_ref[CONV_WIDTH - 1 - d:CONV_WIDTH - d, :] * xs
        lrux_ref[...] = xb[CHUNK - 8:CHUNK]
        yb = yv.astype(BF16)
        rg = _sigmoid(_dot(yb, wa_ref[...]) + ba_ref[...])
        ig = _sigmoid(_dot(yb, wx_ref[...]) + bx_ref[...])
        log_a = -LRU_C * rg * _softplus(-lam_ref[...])
        av = jnp.exp(log_a)
        th = jnp.tanh(log_a)
        gu = jnp.sqrt(-2.0 * th / (1.0 - th)) * (ig * yv)
        step = 1
        while step < CHUNK:
            keep = row_c >= step
            a_sh = jnp.where(keep, pltpu.roll(av, step, axis=0), 1.0)
            u_sh = jnp.where(keep, pltpu.roll(gu, step, axis=0), 0.0)
            gu = av * u_sh + gu
            av = av * a_sh
            step *= 2
        hv = gu + av * lruh_ref[...]
        lruh_ref[...] = hv[CHUNK - 1:CHUNK]
        lru_out = hv * _silu(u_ref[rows, o_lrz:o_lrz + lru_w])
        out_ref[rows, rw_w + hg_w:rw_w + hg_w + lru_w] = lru_out.astype(out_ref.dtype)
        return carry

    lax.fori_loop(0, n_chunks, chunk_body, 0)


def _mixer_constants(rw_w):
    t = np.arange(CHUNK)
    tri = (t[None, :] <= t[:, None])
    same_blk = (t[None, :] // HG_BLOCK) == (t[:, None] // HG_BLOCK)
    bdtri = tri & same_blk
    bdup = (t[None, :] > t[:, None]) & same_blk
    h = np.arange(rw_w) // HEAD
    seg = h[None, :] == h[:, None]
    as_bf16 = lambda m: jnp.asarray(m.astype(np.float32), dtype=BF16)
    return as_bf16(tri), as_bf16(bdtri), as_bf16(bdup), as_bf16(seg)


def _block_diag(w):
    g, n, _ = w.shape
    out = jnp.zeros((g * n, g * n), w.dtype)
    for i in range(g):
        out = out.at[i * n:(i + 1) * n, i * n:(i + 1) * n].set(w[i])
    return out


def _mixer(u, v_first, layer, p, batch, seq):
    n, n_in = u.shape
    rw_w = p["w0"].shape[1]
    hg_w = p["hgnw"].shape[1]
    lru_w = p["lam"].shape[1]
    d_mix = rw_w + hg_w + lru_w
    tb = _row_block(seq, 256)
    assert tb % CHUNK == 0 and rw_w % LANES == 0 and hg_w == rw_w
    nt = seq // tb
    has_vres = v_first is not None

    def tok_spec(width):
        return pl.BlockSpec((tb, width), lambda b, t: (b * nt + t, 0))

    def full_spec(a):
        nd = a.ndim
        return pl.BlockSpec(a.shape, lambda b, t: (0,) * nd)

    args, in_specs = [u], [tok_spec(n_in)]
    if has_vres:
        args.append(v_first)
        in_specs.append(tok_spec(rw_w))
    names = ["mu", "w0", "wup", "a0", "aup"]
    if has_vres:
        names += ["v0", "vdn", "vup"]
    names += ["kk", "ka", "rk", "lnw", "lnb", "lbraw", "hgnw",
              "convw", "convb", "wa", "ba", "wx", "bx", "lam"]
    consts = _mixer_constants(rw_w)
    for a in [p[k] for k in names] + list(consts):
        args.append(a)
        in_specs.append(full_spec(a))

    out_shape = [jax.ShapeDtypeStruct((n, d_mix), BF16)]
    out_specs = [tok_spec(d_mix)]
    if not has_vres:
        out_shape.append(jax.ShapeDtypeStruct((n, rw_w), F32))
        out_specs.append(tok_spec(rw_w))

    n_pairs = rw_w // LANES
    scratch = [pltpu.VMEM((1, 3 * rw_w + LANES), F32),
               pltpu.VMEM((n_pairs, LANES, LANES), F32),
               pltpu.VMEM((hg_w // LANES, LANES, LANES), F32),
               pltpu.VMEM((8, lru_w), F32),
               pltpu.VMEM((1, lru_w), F32)]
    block_bytes = tb * n_in * 4 + tb * d_mix * 2 + 2 * tb * rw_w * 4
    body = functools.partial(_mix_kernel, layer=layer, has_vres=has_vres, n_chunks=tb // CHUNK,
                             rw_w=rw_w, hg_w=hg_w, lru_w=lru_w)
    res = pl.pallas_call(
        body,
        out_shape=out_shape,
        grid=(batch, nt),
        in_specs=in_specs,
        out_specs=out_specs,
        scratch_shapes=scratch,
        compiler_params=pltpu.CompilerParams(
            dimension_semantics=("arbitrary", "arbitrary"), vmem_limit_bytes=_vmem_limit(block_bytes)),
        name=f"mixer_l{layer}",
    )(*args)
    if has_vres:
        return res[0], v_first
    return res[0], res[1]


def kernel(x, norm_w, w_in, rw_mu, rw_w0, rw_w_up, rw_a0, rw_a_up, rw_v0, rw_v_dn, rw_v_up, rw_k_k, rw_k_a, rw_r_k, rw_ln_w, rw_ln_b, hg_lb_raw, hg_norm_w, lru_conv_w, lru_conv_b, lru_wa, lru_ba, lru_wx, lru_bx, lru_lambda, w_out, final_norm_w):
    batch, seq, d = x.shape
    depth = w_in.shape[0]
    rw_w = rw_w0.shape[1]
    lora = rw_w_up.shape[1]
    assert 2 * lora == LANES
    x2 = x.reshape(batch * seq, d)
    zeros_code = jnp.zeros((lora, rw_w), F32)
    row = lambda a: a.reshape(1, -1)
    v_first = None
    for l in range(depth):
        p = {
            "mu": row(rw_mu[l]), "w0": row(rw_w0[l]),
            "wup": jnp.concatenate([rw_w_up[l], zeros_code], axis=0).astype(BF16),
            "a0": row(rw_a0[l]),
            "aup": jnp.concatenate([zeros_code, rw_a_up[l]], axis=0).astype(BF16),
            "kk": row(rw_k_k[l]), "ka": row(rw_k_a[l]), "rk": row(rw_r_k[l]),
            "lnw": row(rw_ln_w[l]), "lnb": row(rw_ln_b[l]),
            "lbraw": hg_lb_raw, "hgnw": row(hg_norm_w[l]),
            "convw": lru_conv_w[l], "convb": row(lru_conv_b[l]),
            "wa": _block_diag(lru_wa[l]).astype(BF16), "ba": row(lru_ba[l]),
            "wx": _block_diag(lru_wx[l]).astype(BF16), "bx": row(lru_bx[l]),
            "lam": row(lru_lambda[l]),
        }
        if l > 0:
            p["v0"] = row(rw_v0[l - 1])
            p["vdn"] = rw_v_dn[l - 1].astype(BF16)
            p["vup"] = rw_v_up[l - 1].astype(BF16)
        u = _inproj(x2, row(norm_w[l]), w_in[l].astype(BF16))
        mix, v_first = _mixer(u, v_first, l, p, batch, seq)
        fw = row(final_norm_w) if l == depth - 1 else None
        x2 = _outproj(x2, mix, w_out[l].astype(BF16), fw)
    return x2.reshape(batch, seq, d)
```

```python
import functools

import numpy as np
import jax
import jax.numpy as jnp
from jax import lax
from jax.experimental import pallas as pl
from jax.experimental.pallas import tpu as pltpu

F32 = jnp.float32
BF16 = jnp.bfloat16

NORM_EPS = 1e-6
RW_GN_EPS = 64e-5
LRU_C = 8.0
HEAD = 64
LANES = 128
CHUNK = 64
HG_BLOCK = 16
CONV_WIDTH = 4
LOG2_HEAD = HEAD.bit_length() - 1
LOG2_CHUNK = CHUNK.bit_length() - 1
V7X_VMEM_BYTES = 64 * 1024 * 1024


def _dot(a, b):
    return jnp.dot(a, b, preferred_element_type=F32)


def _dot_nt(a, b):
    return lax.dot_general(a, b, (((1,), (1,)), ((), ())), preferred_element_type=F32)


def _dot_tn(a, b):
    return lax.dot_general(a, b, (((0,), (0,)), ((), ())), preferred_element_type=F32)


def _mm(a, b):
    return _dot(a.astype(BF16), b.astype(BF16))


def _split2(x):
    hi = x.astype(BF16)
    lo = (x - hi.astype(F32)).astype(BF16)
    return hi, lo


def _split3(x):
    hi = x.astype(BF16)
    r1 = x - hi.astype(F32)
    mid = r1.astype(BF16)
    lo = (r1 - mid.astype(F32)).astype(BF16)
    return hi, mid, lo


def _sigmoid(x):
    return 1.0 / (1.0 + jnp.exp(-x))


def _silu(x):
    return x * _sigmoid(x)


def _softplus(x):
    return jnp.maximum(x, 0.0) + jnp.log1p(jnp.exp(-jnp.abs(x)))


def _iota(shape, dim):
    return lax.broadcasted_iota(jnp.int32, shape, dim)


def _inproj_kernel(x_ref, nw_ref, w_ref, o_ref):
    x = x_ref[...]
    ms = jnp.mean(x * x, axis=-1, keepdims=True)
    h = x * lax.rsqrt(ms + NORM_EPS) * nw_ref[...]
    o_ref[...] = _dot(h.astype(BF16), w_ref[...])


def _outproj_kernel(x_ref, m_ref, w_ref, o_ref):
    o_ref[...] = x_ref[...] + _dot(m_ref[...], w_ref[...])


def _outproj_final_kernel(x_ref, m_ref, w_ref, fw_ref, o_ref):
    x = x_ref[...] + _dot(m_ref[...], w_ref[...])
    ms = jnp.mean(x * x, axis=-1, keepdims=True)
    o_ref[...] = x * lax.rsqrt(ms + NORM_EPS) * fw_ref[...]


def _row_block(n_rows, target):
    blk = min(target, n_rows)
    while n_rows % blk:
        blk //= 2
    return blk


def _vmem_limit(block_bytes):
    return int(min(V7X_VMEM_BYTES * 7 // 8, 2 * block_bytes + 16 * 1024 * 1024))


def _inproj(x2, norm_w, w_bf16):
    n, d = x2.shape
    n_in = w_bf16.shape[1]
    tm = _row_block(n, 256)
    block_bytes = tm * d * 4 + d * n_in * 2 + tm * n_in * 4
    return pl.pallas_call(
        _inproj_kernel,
        out_shape=jax.ShapeDtypeStruct((n, n_in), F32),
        grid=(n // tm,),
        in_specs=[pl.BlockSpec((tm, d), lambda i: (i, 0)),
                  pl.BlockSpec((1, d), lambda i: (0, 0)),
                  pl.BlockSpec((d, n_in), lambda i: (0, 0))],
        out_specs=pl.BlockSpec((tm, n_in), lambda i: (i, 0)),
        compiler_params=pltpu.CompilerParams(
            dimension_semantics=("arbitrary",), vmem_limit_bytes=_vmem_limit(block_bytes)),
        name="inproj",
    )(x2, norm_w, w_bf16)


def _outproj(x2, mix, w_bf16, final_w=None):
    n, d = x2.shape
    dm = mix.shape[1]
    tm = _row_block(n, 512)
    block_bytes = 2 * tm * d * 4 + tm * dm * 2 + dm * d * 2
    in_specs = [pl.BlockSpec((tm, d), lambda i: (i, 0)),
                pl.BlockSpec((tm, dm), lambda i: (i, 0)),
                pl.BlockSpec((dm, d), lambda i: (0, 0))]
    args = [x2, mix, w_bf16]
    body = _outproj_kernel
    if final_w is not None:
        in_specs.append(pl.BlockSpec((1, d), lambda i: (0, 0)))
        args.append(final_w)
        body = _outproj_final_kernel
    return pl.pallas_call(
        body,
        out_shape=jax.ShapeDtypeStruct((n, d), F32),
        grid=(n // tm,),
        in_specs=in_specs,
        out_specs=pl.BlockSpec((tm, d), lambda i: (i, 0)),
        compiler_params=pltpu.CompilerParams(
            dimension_semantics=("arbitrary",), vmem_limit_bytes=_vmem_limit(block_bytes)),
        name="outproj_final" if final_w is not None else "outproj",
    )(*args)


def _stack_heads(x, lane_head0):
    return jnp.concatenate([jnp.where(lane_head0, x, 0.0), jnp.where(lane_head0, 0.0, x)], axis=0)


def _mix_kernel(*refs, layer, has_vres, n_chunks, nb, rw_w, hg_w, lru_w):
    it = iter(refs)
    u_ref = next(it)
    vfirst_in_ref = next(it) if has_vres else None
    mu_ref, w0_ref, wup_ref, a0_ref, aup_ref = (next(it) for _ in range(5))
    if has_vres:
        v0_ref, vdn_ref, vup_ref = (next(it) for _ in range(3))
    kk_ref, ka_ref, rk_ref, lnw_ref, lnb_ref = (next(it) for _ in range(5))
    lbraw_ref, hgnw_ref = (next(it) for _ in range(2))
    convw_ref, convb_ref, wa_ref, ba_ref, wx_ref, bx_ref, lam_ref = (next(it) for _ in range(7))
    tri3_ref, hg2_ref, ones2_ref, seg256_ref = (next(it) for _ in range(4))
    out_ref = next(it)
    vfirst_out_ref = None if has_vres else next(it)
    rwprev_ref, rwstate_ref, hgstate_ref, lrux_ref, lruh_ref = (next(it) for _ in range(5))

    n_pairs = rw_w // LANES
    rows_all = nb * CHUNK
    c2 = 2 * CHUNK
    rw_shift = 3 * rw_w + LANES
    o_rwz = rw_shift
    o_hgq = o_rwz + rw_w
    o_hgf = o_hgq + hg_w
    o_hgi = o_hgf + hg_w
    o_hgz = o_hgi + hg_w
    o_lrx = o_hgz + hg_w
    o_lrz = o_lrx + lru_w

    @pl.when(pl.program_id(0) == 0)
    def _reset():
        rwprev_ref[...] = jnp.zeros_like(rwprev_ref)
        rwstate_ref[...] = jnp.zeros_like(rwstate_ref)
        hgstate_ref[...] = jnp.zeros_like(hgstate_ref)
        lrux_ref[...] = jnp.zeros_like(lrux_ref)
        lruh_ref[...] = jnp.zeros_like(lruh_ref)

    row = _iota((rows_all, 1), 0)
    tpos = row & (CHUNK - 1)
    lane_head0 = _iota((1, LANES), 1) < HEAD
    si = _iota((c2, c2), 0)
    sj = _iota((c2, c2), 1)
    same_head = (si >> LOG2_CHUNK) == (sj >> LOG2_CHUNK)
    ti = si & (CHUNK - 1)
    tj = sj & (CHUNK - 1)
    strict_lower = same_head & (tj < ti)
    incl_lower = same_head & (tj <= ti)
    eye = jnp.where(si == sj, 1.0, 0.0)
    bd_state = (_iota((LANES, LANES), 0) >> LOG2_HEAD) == (_iota((LANES, LANES), 1) >> LOG2_HEAD)
    insts = [(b, p) for p in range(n_pairs) for b in range(nb)]

    ones2 = ones2_ref[...]
    seg256 = seg256_ref[...]
    seg128 = seg256[0:LANES, 0:LANES]

    def segsum(x):
        hi, lo = _split2(x)
        return jnp.concatenate(
            [_dot(jnp.concatenate([hi[:, p * LANES:(p + 1) * LANES], lo[:, p * LANES:(p + 1) * LANES]], axis=1), ones2)
             for p in range(x.shape[1] // LANES)], axis=1)

    def head_sums_bf16(x):
        return jnp.concatenate([_dot(x[:, 0:2 * LANES], seg256), _dot(x[:, 2 * LANES:3 * LANES], seg128)], axis=1)

    def per_batch(ref):
        val = ref[0]
        for b in range(1, nb):
            val = jnp.where(row >= b * CHUNK, ref[b], val)
        return val

    lbraw = lbraw_ref[...]
    lbe = jnp.exp(lbraw - jnp.max(lbraw, axis=0, keepdims=True))
    lbw = lbe / jnp.sum(lbe, axis=0, keepdims=True)
    lb = lbw[0:1, :]
    for j in range(1, layer + 1):
        lb = lb + lbw[j:j + 1, :]
    lb = lb - lbw[0:1, :]

    def chunk_body(c, carry):
        r0 = pl.multiple_of(c * CHUNK, CHUNK)
        rows = pl.ds(r0, CHUNK)

        def load(lo, hi):
            return jnp.concatenate([u_ref[b, rows, lo:hi] for b in range(nb)], axis=0)

        def store(lo, val):
            for b in range(nb):
                out_ref[b, rows, lo:lo + val.shape[1]] = val[b * CHUNK:(b + 1) * CHUNK].astype(out_ref.dtype)

        def blk(x, b, p):
            return x[b * CHUNK:(b + 1) * CHUNK, p * LANES:(p + 1) * LANES]

        ru = load(0, rw_shift)
        sh = jnp.where(tpos == 0, per_batch(rwprev_ref), pltpu.roll(ru, 1, axis=0))
        for b in range(nb):
            rwprev_ref[b] = ru[(b + 1) * CHUNK - 1:(b + 1) * CHUNK, :]
        ul = ru + mu_ref[...] * (sh - ru)
        r = ul[:, 0:rw_w]
        k = ul[:, rw_w:2 * rw_w]
        v = ul[:, 2 * rw_w:3 * rw_w]
        codes = ul[:, 3 * rw_w:rw_shift]
        w_log = -_softplus(-(w0_ref[...] + _mm(jnp.tanh(codes), wup_ref[...]))) - 0.5
        nlw = jnp.exp(w_log)
        a = _sigmoid(a0_ref[...] + _mm(codes, aup_ref[...]))
        if has_vres:
            vf = jnp.concatenate([vfirst_in_ref[b, rows, :] for b in range(nb)], axis=0)
            gate = _sigmoid(v0_ref[...] + _mm(_mm(v, vdn_ref[...]), vup_ref[...]))
            v = v + (vf - v) * gate
        else:
            for b in range(nb):
                vfirst_out_ref[b, rows, :] = v[b * CHUNK:(b + 1) * CHUNK]
        kk = k * kk_ref[...]
        kk = kk / jnp.maximum(jnp.sqrt(segsum(kk * kk)), 1e-12)
        k2 = k * (1.0 + (a - 1.0) * ka_ref[...])
        a_in = -kk
        b_in = kk * a
        bonus = segsum(r * k2 * rk_ref[...]) * v

        cs = _dot(tri3_ref[...], jnp.concatenate(_split3(nlw), axis=0))
        e_pos = jnp.exp(-cs)
        e_neg = jnp.exp(cs)
        e_prev = jnp.exp(nlw - cs)
        a_t = a_in * e_prev
        r_t = r * e_pos
        b_t = b_in * e_neg
        k_t = k2 * e_neg

        lhs, rhs, vs, st, am = {}, {}, {}, {}, {}
        a_ak, l_rb, l_rk, pw, tinv = {}, {}, {}, {}, {}
        for i in insts:
            lhs[i] = jnp.concatenate([_stack_heads(blk(a_t, *i), lane_head0),
                                      _stack_heads(blk(r_t, *i), lane_head0)], axis=0).astype(BF16)
            rhs[i] = jnp.concatenate([_stack_heads(blk(b_t, *i), lane_head0),
                                      _stack_heads(blk(k_t, *i), lane_head0)], axis=0).astype(BF16)
            vs[i] = _stack_heads(blk(v, *i), lane_head0)
        for i in insts:
            sc = _dot_nt(lhs[i], rhs[i])
            pw[i] = jnp.where(strict_lower, sc[0:c2, 0:c2], 0.0)
            a_ak[i] = jnp.where(strict_lower, sc[0:c2, c2:2 * c2], 0.0)
            l_rb[i] = jnp.where(incl_lower, sc[c2:2 * c2, 0:c2], 0.0)
            l_rk[i] = jnp.where(incl_lower, sc[c2:2 * c2, c2:2 * c2], 0.0)
        for i in insts:
            st[i] = rwstate_ref[i[0] * n_pairs + i[1]]
            am[i] = _dot_nt(lhs[i], st[i].astype(BF16))
        for i in insts:
            tinv[i] = eye + pw[i]
            pw[i] = _mm(pw[i], pw[i])
        for s in range(1, 5):
            for i in insts:
                both = _mm(jnp.concatenate([pw[i], tinv[i]], axis=0), pw[i])
                pw[i] = both[0:c2]
                tinv[i] = tinv[i] + both[c2:2 * c2]
        for i in insts:
            tinv[i] = tinv[i] + _mm(tinv[i], pw[i])
        x = {i: am[i][0:c2] + _mm(a_ak[i], vs[i]) for i in insts}
        uv = {i: jnp.concatenate([_mm(tinv[i], x[i]), vs[i]], axis=0).astype(BF16) for i in insts}
        yb = {}
        for i in insts:
            yy = am[i][c2:2 * c2] + _dot(jnp.concatenate([l_rb[i], l_rk[i]], axis=1).astype(BF16), uv[i])
            yb[i] = yy[0:CHUNK] + yy[CHUNK:c2]
        for i in insts:
            b, p = i
            last = (b + 1) * CHUNK - 1
            rwstate_ref[b * n_pairs + p] = (st[i] + _dot_tn(uv[i], rhs[i])) * e_pos[last:last + 1, p * LANES:(p + 1) * LANES]
        y = jnp.concatenate([jnp.concatenate([yb[(b, p)] for p in range(n_pairs)], axis=1) for b in range(nb)], axis=0)
        mean = segsum(y) * (1.0 / HEAD)
        yc = y - mean
        var = segsum(yc * yc) * (1.0 / HEAD)
        yn = yc * lax.rsqrt(var + RW_GN_EPS) * lnw_ref[...] + lnb_ref[...]
        store(0, (yn + bonus) * _silu(load(o_rwz, o_rwz + rw_w)))

        q = load(o_hgq, o_hgq + hg_w)
        f = lb + (1.0 - lb) * _sigmoid(load(o_hgf, o_hgf + hg_w))
        g = jnp.log(f)
        kx = 1.0 - f
        iv = load(o_hgi, o_hgi + hg_w)
        bb = _dot(hg2_ref[...], jnp.concatenate(_split3(g), axis=0))
        bcum = bb[0:rows_all]
        brev = bb[rows_all:2 * rows_all]
        pos = row & (HG_BLOCK - 1)
        acc = head_sums_bf16((q * kx).astype(BF16)) * iv
        for d in range(1, HG_BLOCK):
            valid = pos >= d
            dd = jnp.where(valid, bcum - pltpu.roll(bcum, d, axis=0), 0.0)
            wgt = jnp.where(valid, q * jnp.exp(dd) * pltpu.roll(kx, d, axis=0), 0.0)
            acc = acc + head_sums_bf16(wgt.astype(BF16)) * pltpu.roll(iv, d, axis=0)
        qh = (q * jnp.exp(bcum)).astype(BF16)
        kh = (kx * jnp.exp(brev)).astype(BF16)
        ivb = iv.astype(BF16)
        hst = {i: hgstate_ref[i[0] * n_pairs + i[1]] for i in insts}
        oblk = {i: [] for i in insts}
        for j in range(CHUNK // HG_BLOCK):
            for i in insts:
                b, p = i
                rs = slice(b * CHUNK + j * HG_BLOCK, b * CHUNK + (j + 1) * HG_BLOCK)
                ls = slice(p * LANES, (p + 1) * LANES)
                oblk[i].append(_dot_nt(qh[rs, ls], hst[i].astype(BF16)))
                upd = _dot_tn(ivb[rs, ls], kh[rs, ls])
                last = b * CHUNK + (j + 1) * HG_BLOCK - 1
                hst[i] = hst[i] * jnp.exp(bcum[last:last + 1, ls]) + jnp.where(bd_state, upd, 0.0)
        for i in insts:
            hgstate_ref[i[0] * n_pairs + i[1]] = hst[i]
        o = acc + jnp.concatenate(
            [jnp.concatenate([jnp.concatenate(oblk[(b, p)], axis=0) for p in range(n_pairs)], axis=1)
             for b in range(nb)], axis=0)
        ms = segsum(o * o) * (1.0 / HEAD)
        store(rw_w, o * lax.rsqrt(ms + NORM_EPS) * hgnw_ref[...] * _silu(load(o_hgz, o_hgz + hg_w)))

        xb = load(o_lrx, o_lrx + lru_w)
        row8 = _iota((8, 1), 0)
        yv = convb_ref[...] + convw_ref[CONV_WIDTH - 1:CONV_WIDTH, :] * xb
        for d in range(1, CONV_WIDTH):
            rolled = pltpu.roll(xb, d, axis=0)
            pieces = []
            for b in range(nb):
                tail = pltpu.roll(lrux_ref[b], d, axis=0)
                pieces.append(jnp.where(row8 < d, tail, rolled[b * CHUNK:b * CHUNK + 8]))
                pieces.append(rolled[b * CHUNK + 8:(b + 1) * CHUNK])
            yv = yv + convw_ref[CONV_WIDTH - 1 - d:CONV_WIDTH - d, :] * jnp.concatenate(pieces, axis=0)
        for b in range(nb):
            lrux_ref[b] = xb[(b + 1) * CHUNK - 8:(b + 1) * CHUNK]
        ybf = yv.astype(BF16)
        rg = _sigmoid(_dot(ybf, wa_ref[...]) + ba_ref[...])
        ig = _sigmoid(_dot(ybf, wx_ref[...]) + bx_ref[...])
        log_a = -LRU_C * rg * _softplus(-lam_ref[...])
        av = jnp.exp(log_a)
        th = jnp.tanh(log_a)
        gu = jnp.sqrt(-2.0 * th / (1.0 - th)) * (ig * yv)
        step = 1
        while step < CHUNK:
            keep = tpos >= step
            a_sh = jnp.where(keep, pltpu.roll(av, step, axis=0), 1.0)
            u_sh = jnp.where(keep, pltpu.roll(gu, step, axis=0), 0.0)
            gu = av * u_sh + gu
            av = av * a_sh
            step *= 2
        hv = gu + av * per_batch(lruh_ref)
        for b in range(nb):
            lruh_ref[b] = hv[(b + 1) * CHUNK - 1:(b + 1) * CHUNK]
        store(rw_w + hg_w, hv * _silu(load(o_lrz, o_lrz + lru_w)))
        return carry

    lax.fori_loop(0, n_chunks, chunk_body, 0)


def _mixer_constants(nb):
    rows_all = nb * CHUNK
    t = np.arange(rows_all)
    same_chunk = (t[None, :] // CHUNK) == (t[:, None] // CHUNK)
    same_blk = (t[None, :] // HG_BLOCK) == (t[:, None] // HG_BLOCK)
    lower = t[None, :] <= t[:, None]
    tri3 = np.tile(lower & same_chunk, (1, 3))
    hg2 = np.tile(np.concatenate([lower & same_blk, (~lower) & same_blk], axis=0), (1, 3))
    h = np.arange(2 * LANES) // HEAD
    seg256 = h[None, :] == h[:, None]
    ones2 = np.tile(seg256[0:LANES, 0:LANES], (2, 1))
    as_bf16 = lambda m: jnp.asarray(m.astype(np.float32), dtype=BF16)
    return as_bf16(tri3), as_bf16(hg2), as_bf16(ones2), as_bf16(seg256)


def _block_diag(w):
    g, n, _ = w.shape
    out = jnp.zeros((g * n, g * n), w.dtype)
    for i in range(g):
        out = out.at[i * n:(i + 1) * n, i * n:(i + 1) * n].set(w[i])
    return out


def _mixer(u, v_first, layer, p, batch, seq):
    n_in = u.shape[1]
    rw_w = p["w0"].shape[1]
    hg_w = p["hgnw"].shape[1]
    lru_w = p["lam"].shape[1]
    d_mix = rw_w + hg_w + lru_w
    tb = _row_block(seq, 256)
    assert tb % CHUNK == 0 and rw_w == 3 * LANES and hg_w == rw_w
    has_vres = v_first is not None

    def tok_spec(width):
        return pl.BlockSpec((batch, tb, width), lambda t: (0, t, 0))

    def full_spec(a):
        nd = a.ndim
        return pl.BlockSpec(a.shape, lambda t: (0,) * nd)

    args, in_specs = [u.reshape(batch, seq, n_in)], [tok_spec(n_in)]
    if has_vres:
        args.append(v_first)
        in_specs.append(tok_spec(rw_w))
    names = ["mu", "w0", "wup", "a0", "aup"]
    if has_vres:
        names += ["v0", "vdn", "vup"]
    names += ["kk", "ka", "rk", "lnw", "lnb", "lbraw", "hgnw",
              "convw", "convb", "wa", "ba", "wx", "bx", "lam"]
    for a in [p[k] for k in names] + list(_mixer_constants(batch)):
        args.append(a)
        in_specs.append(full_spec(a))

    out_shape = [jax.ShapeDtypeStruct((batch, seq, d_mix), BF16)]
    out_specs = [tok_spec(d_mix)]
    if not has_vres:
        out_shape.append(jax.ShapeDtypeStruct((batch, seq, rw_w), F32))
        out_specs.append(tok_spec(rw_w))

    n_pairs = rw_w // LANES
    scratch = [pltpu.VMEM((batch, 1, 3 * rw_w + LANES), F32),
               pltpu.VMEM((batch * n_pairs, LANES, LANES), F32),
               pltpu.VMEM((batch * n_pairs, LANES, LANES), F32),
               pltpu.VMEM((batch, 8, lru_w), F32),
               pltpu.VMEM((batch, 1, lru_w), F32)]
    block_bytes = batch * tb * (n_in * 4 + d_mix * 2 + 2 * rw_w * 4)
    body = functools.partial(_mix_kernel, layer=layer, has_vres=has_vres, n_chunks=tb // CHUNK, nb=batch,
                             rw_w=rw_w, hg_w=hg_w, lru_w=lru_w)
    res = pl.pallas_call(
        body,
        out_shape=out_shape,
        grid=(seq // tb,),
        in_specs=in_specs,
        out_specs=out_specs,
        scratch_shapes=scratch,
        compiler_params=pltpu.CompilerParams(
            dimension_semantics=("arbitrary",), vmem_limit_bytes=_vmem_limit(block_bytes)),
        name=f"mixer_l{layer}",
    )(*args)
    mix = res[0].reshape(batch * seq, d_mix)
    if has_vres:
        return mix, v_first
    return mix, res[1]


def kernel(x, norm_w, w_in, rw_mu, rw_w0, rw_w_up, rw_a0, rw_a_up, rw_v0, rw_v_dn, rw_v_up, rw_k_k, rw_k_a, rw_r_k, rw_ln_w, rw_ln_b, hg_lb_raw, hg_norm_w, lru_conv_w, lru_conv_b, lru_wa, lru_ba, lru_wx, lru_bx, lru_lambda, w_out, final_norm_w):
    batch, seq, d = x.shape
    depth = w_in.shape[0]
    rw_w = rw_w0.shape[1]
    lora = rw_w_up.shape[1]
    assert 2 * lora == LANES
    x2 = x.reshape(batch * seq, d)
    zeros_code = jnp.zeros((lora, rw_w), F32)
    row = lambda a: a.reshape(1, -1)
    v_first = None
    for l in range(depth):
        p = {
            "mu": row(rw_mu[l]), "w0": row(rw_w0[l]),
            "wup": jnp.concatenate([rw_w_up[l], zeros_code], axis=0).astype(BF16),
            "a0": row(rw_a0[l]),
            "aup": jnp.concatenate([zeros_code, rw_a_up[l]], axis=0).astype(BF16),
            "kk": row(rw_k_k[l]), "ka": row(rw_k_a[l]), "rk": row(rw_r_k[l]),
            "lnw": row(rw_ln_w[l]), "lnb": row(rw_ln_b[l]),
            "lbraw": hg_lb_raw, "hgnw": row(hg_norm_w[l]),
            "convw": lru_conv_w[l], "convb": row(lru_conv_b[l]),
            "wa": _block_diag(lru_wa[l]).astype(BF16), "ba": row(lru_ba[l]),
            "wx": _block_diag(lru_wx[l]).astype(BF16), "bx": row(lru_bx[l]),
            "lam": row(lru_lambda[l]),
        }
        if l > 0:
            p["v0"] = row(rw_v0[l - 1])
            p["vdn"] = rw_v_dn[l - 1].astype(BF16)
            p["vup"] = rw_v_up[l - 1].astype(BF16)
        u = _inproj(x2, row(norm_w[l]), w_in[l].astype(BF16))
        mix, v_first = _mixer(u, v_first, l, p, batch, seq)
        fw = row(final_norm_w) if l == depth - 1 else None
        x2 = _outproj(x2, mix, w_out[l].astype(BF16), fw)
    return x2.reshape(batch, seq, d)
```

```python
import functools

import numpy as np
import jax
import jax.numpy as jnp
from jax import lax
from jax.experimental import pallas as pl
from jax.experimental.pallas import tpu as pltpu

F32 = jnp.float32
BF16 = jnp.bfloat16

NORM_EPS = 1e-6
RW_GN_EPS = 64e-5
LRU_C = 8.0
HEAD = 64
LANES = 128
CHUNK = 64
HG_SUB = 8
HG_LEVELS = (16, 32, 64)
CONV_WIDTH = 4
LOG2_HEAD = HEAD.bit_length() - 1
LOG2_CHUNK = CHUNK.bit_length() - 1
V7X_VMEM_BYTES = 64 * 1024 * 1024


def _dot(a, b):
    return jnp.dot(a, b, preferred_element_type=F32)


def _dot_nt(a, b):
    return lax.dot_general(a, b, (((1,), (1,)), ((), ())), preferred_element_type=F32)


def _dot_tn(a, b):
    return lax.dot_general(a, b, (((0,), (0,)), ((), ())), preferred_element_type=F32)


def _mm(a, b):
    return _dot(a.astype(BF16), b.astype(BF16))


def _split2(x):
    hi = x.astype(BF16)
    lo = (x - hi.astype(F32)).astype(BF16)
    return hi, lo


def _sigmoid(x):
    return 1.0 / (1.0 + jnp.exp(-x))


def _silu(x):
    return x * _sigmoid(x)


def _softplus(x):
    return jnp.maximum(x, 0.0) + jnp.log1p(jnp.exp(-jnp.abs(x)))


def _iota(shape, dim):
    return lax.broadcasted_iota(jnp.int32, shape, dim)


def _inproj_kernel(x_ref, nw_ref, w_ref, o_ref):
    x = x_ref[...]
    ms = jnp.mean(x * x, axis=-1, keepdims=True)
    h = x * lax.rsqrt(ms + NORM_EPS) * nw_ref[...]
    o_ref[...] = _dot(h.astype(BF16), w_ref[...])


def _outproj_kernel(x_ref, m_ref, w_ref, o_ref):
    o_ref[...] = x_ref[...] + _dot(m_ref[...], w_ref[...])


def _outproj_final_kernel(x_ref, m_ref, w_ref, fw_ref, o_ref):
    x = x_ref[...] + _dot(m_ref[...], w_ref[...])
    ms = jnp.mean(x * x, axis=-1, keepdims=True)
    o_ref[...] = x * lax.rsqrt(ms + NORM_EPS) * fw_ref[...]


def _row_block(n_rows, target):
    blk = min(target, n_rows)
    while n_rows % blk:
        blk //= 2
    return blk


def _vmem_limit(block_bytes):
    return int(min(V7X_VMEM_BYTES * 7 // 8, 2 * block_bytes + 16 * 1024 * 1024))


def _inproj(x2, norm_w, w_bf16):
    n, d = x2.shape
    n_in = w_bf16.shape[1]
    tm = _row_block(n, 256)
    block_bytes = tm * d * 4 + d * n_in * 2 + tm * n_in * 4
    return pl.pallas_call(
        _inproj_kernel,
        out_shape=jax.ShapeDtypeStruct((n, n_in), F32),
        grid=(n // tm,),
        in_specs=[pl.BlockSpec((tm, d), lambda i: (i, 0)),
                  pl.BlockSpec((1, d), lambda i: (0, 0)),
                  pl.BlockSpec((d, n_in), lambda i: (0, 0))],
        out_specs=pl.BlockSpec((tm, n_in), lambda i: (i, 0)),
        compiler_params=pltpu.CompilerParams(
            dimension_semantics=("arbitrary",), vmem_limit_bytes=_vmem_limit(block_bytes)),
        name="inproj",
    )(x2, norm_w, w_bf16)


def _outproj(x2, mix, w_bf16, final_w=None):
    n, d = x2.shape
    dm = mix.shape[1]
    tm = _row_block(n, 512)
    block_bytes = 2 * tm * d * 4 + tm * dm * 2 + dm * d * 2
    in_specs = [pl.BlockSpec((tm, d), lambda i: (i, 0)),
                pl.BlockSpec((tm, dm), lambda i: (i, 0)),
                pl.BlockSpec((dm, d), lambda i: (0, 0))]
    args = [x2, mix, w_bf16]
    body = _outproj_kernel
    if final_w is not None:
        in_specs.append(pl.BlockSpec((1, d), lambda i: (0, 0)))
        args.append(final_w)
        body = _outproj_final_kernel
    return pl.pallas_call(
        body,
        out_shape=jax.ShapeDtypeStruct((n, d), F32),
        grid=(n // tm,),
        in_specs=in_specs,
        out_specs=pl.BlockSpec((tm, d), lambda i: (i, 0)),
        compiler_params=pltpu.CompilerParams(
            dimension_semantics=("arbitrary",), vmem_limit_bytes=_vmem_limit(block_bytes)),
        name="outproj_final" if final_w is not None else "outproj",
    )(*args)


def _stack_heads(x, lane_head0):
    return jnp.concatenate([jnp.where(lane_head0, x, 0.0), jnp.where(lane_head0, 0.0, x)], axis=0)


def _mix_kernel(*refs, layer, has_vres, n_chunks, nb, rw_w, hg_w, lru_w):
    it = iter(refs)
    u_ref = next(it)
    vfirst_in_ref = next(it) if has_vres else None
    mu_ref, w0_ref, wup_ref, a0_ref, aup_ref = (next(it) for _ in range(5))
    if has_vres:
        v0_ref, vdn_ref, vup_ref = (next(it) for _ in range(3))
    kk_ref, ka_ref, rk_ref, lnw_ref, lnb_ref = (next(it) for _ in range(5))
    lbraw_ref, hgnw_ref = (next(it) for _ in range(2))
    convw_ref, convb_ref, wa_ref, ba_ref, wx_ref, bx_ref, lam_ref = (next(it) for _ in range(7))
    tri2_ref, hgm_ref, seg256_ref = (next(it) for _ in range(3))
    out_ref = next(it)
    vfirst_out_ref = None if has_vres else next(it)
    rwprev_ref, rwstate_ref, hgstate_ref, lrux_ref, lruh_ref = (next(it) for _ in range(5))

    n_pairs = rw_w // LANES
    rows_all = nb * CHUNK
    c2 = 2 * CHUNK
    rw_shift = 3 * rw_w + LANES
    o_rwz = rw_shift
    o_hgq = o_rwz + rw_w
    o_hgf = o_hgq + hg_w
    o_hgi = o_hgf + hg_w
    o_hgz = o_hgi + hg_w
    o_lrx = o_hgz + hg_w
    o_lrz = o_lrx + lru_w

    @pl.when(pl.program_id(0) == 0)
    def _reset():
        rwprev_ref[...] = jnp.zeros_like(rwprev_ref)
        rwstate_ref[...] = jnp.zeros_like(rwstate_ref)
        hgstate_ref[...] = jnp.zeros_like(hgstate_ref)
        lrux_ref[...] = jnp.zeros_like(lrux_ref)
        lruh_ref[...] = jnp.zeros_like(lruh_ref)

    row = _iota((rows_all, 1), 0)
    tpos = row & (CHUNK - 1)
    lane_head0 = _iota((1, LANES), 1) < HEAD
    si = _iota((c2, c2), 0)
    sj = _iota((c2, c2), 1)
    same_head = (si >> LOG2_CHUNK) == (sj >> LOG2_CHUNK)
    ti = si & (CHUNK - 1)
    tj = sj & (CHUNK - 1)
    strict_lower = same_head & (tj < ti)
    incl_lower = same_head & (tj <= ti)
    eye = jnp.where(si == sj, 1.0, 0.0)
    same_group = {size: (si >> (size.bit_length() - 1)) == (sj >> (size.bit_length() - 1)) for size in HG_LEVELS}
    bd_state = (_iota((LANES, LANES), 0) >> LOG2_HEAD) == (_iota((LANES, LANES), 1) >> LOG2_HEAD)
    insts = [(b, p) for p in range(n_pairs) for b in range(nb)]

    seg256 = seg256_ref[...]
    seg128 = seg256[0:LANES, 0:LANES]

    def segsum(x):
        xb = x.astype(BF16)
        return jnp.concatenate([_dot(xb[:, 0:2 * LANES], seg256), _dot(xb[:, 2 * LANES:3 * LANES], seg128)], axis=1)

    def per_batch(ref):
        val = ref[0]
        for b in range(1, nb):
            val = jnp.where(row >= b * CHUNK, ref[b], val)
        return val

    lbraw = lbraw_ref[...]
    lbe = jnp.exp(lbraw - jnp.max(lbraw, axis=0, keepdims=True))
    lbw = lbe / jnp.sum(lbe, axis=0, keepdims=True)
    lb = lbw[0:1, :]
    for j in range(1, layer + 1):
        lb = lb + lbw[j:j + 1, :]
    lb = lb - lbw[0:1, :]

    def blk(x, b, p):
        return x[b * CHUNK:(b + 1) * CHUNK, p * LANES:(p + 1) * LANES]

    def chunk_body(c, carry):
        r0 = pl.multiple_of(c * CHUNK, CHUNK)
        rows = pl.ds(r0, CHUNK)

        def load(lo, hi):
            return jnp.concatenate([u_ref[b, rows, lo:hi] for b in range(nb)], axis=0)

        def store(lo, val):
            for b in range(nb):
                out_ref[b, rows, lo:lo + val.shape[1]] = val[b * CHUNK:(b + 1) * CHUNK].astype(out_ref.dtype)

        def rwkv_stages():
            ru = jnp.concatenate([u_ref[b, rows, 0:rw_shift] for b in range(nb)], axis=0)
            sh = jnp.where(tpos == 0, per_batch(rwprev_ref), pltpu.roll(ru, 1, axis=0))
            for b in range(nb):
                rwprev_ref[b] = ru[(b + 1) * CHUNK - 1:(b + 1) * CHUNK, :]
            ul = ru + mu_ref[...] * (sh - ru)
            r = ul[:, 0:rw_w]
            k = ul[:, rw_w:2 * rw_w]
            v = ul[:, 2 * rw_w:3 * rw_w]
            codes = ul[:, 3 * rw_w:rw_shift]
            w_log = -_softplus(-(w0_ref[...] + _mm(jnp.tanh(codes), wup_ref[...]))) - 0.5
            nlw = jnp.exp(w_log)
            a = _sigmoid(a0_ref[...] + _mm(codes, aup_ref[...]))
            if has_vres:
                vf = jnp.concatenate([vfirst_in_ref[b, rows, :] for b in range(nb)], axis=0)
                gate = _sigmoid(v0_ref[...] + _mm(_mm(v, vdn_ref[...]), vup_ref[...]))
                v = v + (vf - v) * gate
            kk = k * kk_ref[...]
            kk = kk / jnp.maximum(jnp.sqrt(segsum(kk * kk)), 1e-12)
            k2 = k * (1.0 + (a - 1.0) * ka_ref[...])
            a_in = -kk
            b_in = kk * a
            bonus = segsum(r * k2 * rk_ref[...]) * v

            cs = _dot(tri2_ref[...], jnp.concatenate(_split2(nlw), axis=0))
            e_pos = jnp.exp(-cs)
            e_neg = jnp.exp(cs)
            e_prev = jnp.exp(nlw - cs)
            a_t = a_in * e_prev
            r_t = r * e_pos
            b_t = b_in * e_neg
            k_t = k2 * e_neg

            lhs, rhs, vs, a_ak, lrk, avs, pw, tinv = {}, {}, {}, {}, {}, {}, {}, {}
            for i in insts:
                lhs[i] = jnp.concatenate([_stack_heads(blk(a_t, *i), lane_head0),
                                          _stack_heads(blk(r_t, *i), lane_head0)], axis=0).astype(BF16)
                rhs[i] = jnp.concatenate([_stack_heads(blk(b_t, *i), lane_head0),
                                          _stack_heads(blk(k_t, *i), lane_head0)], axis=0).astype(BF16)
                vs[i] = _stack_heads(blk(v, *i), lane_head0).astype(BF16)
            yield
            for i in insts:
                sc = _dot_nt(lhs[i], rhs[i])
                pw[i] = jnp.where(strict_lower, sc[0:c2, 0:c2], 0.0)
                a_ak[i] = jnp.where(strict_lower, sc[0:c2, c2:2 * c2], 0.0).astype(BF16)
                lrk[i] = jnp.concatenate([jnp.where(incl_lower, sc[c2:2 * c2, 0:c2], 0.0),
                                          jnp.where(incl_lower, sc[c2:2 * c2, c2:2 * c2], 0.0)], axis=1).astype(BF16)
            yield
            for i in insts:
                avs[i] = _dot(a_ak[i], vs[i])
                tinv[i] = eye + pw[i]
                pw[i] = _mm(pw[i], pw[i])
            yield
            for s in range(1, 5):
                for i in insts:
                    both = _mm(jnp.concatenate([pw[i], tinv[i]], axis=0), pw[i])
                    pw[i] = both[0:c2]
                    tinv[i] = tinv[i] + both[c2:2 * c2]
                yield
            for i in insts:
                tinv[i] = (tinv[i] + _mm(tinv[i], pw[i])).astype(BF16)
            yield
            st, am, uv, yb = {}, {}, {}, {}
            for n, i in enumerate(insts):
                st[i] = rwstate_ref[n]
                am[i] = _dot_nt(lhs[i], st[i].astype(BF16))
            yield
            for i in insts:
                x = am[i][0:c2] + avs[i]
                uv[i] = jnp.concatenate([_dot(tinv[i], x.astype(BF16)).astype(BF16), vs[i]], axis=0)
            yield
            for i in insts:
                yy = am[i][c2:2 * c2] + _dot(lrk[i], uv[i])
                yb[i] = yy[0:CHUNK] + yy[CHUNK:c2]
            yield
            for n, i in enumerate(insts):
                b, p = i
                last = (b + 1) * CHUNK - 1
                rwstate_ref[n] = (st[i] + _dot_tn(uv[i], rhs[i])) * e_pos[last:last + 1, p * LANES:(p + 1) * LANES]
            yield
            y = jnp.concatenate([jnp.concatenate([yb[(b, p)] for p in range(n_pairs)], axis=1) for b in range(nb)], axis=0)
            mean = segsum(y) * (1.0 / HEAD)
            yc = y - mean
            yield
            var = segsum(yc * yc) * (1.0 / HEAD)
            yn = yc * lax.rsqrt(var + RW_GN_EPS) * lnw_ref[...] + lnb_ref[...]
            if not has_vres:
                for b in range(nb):
                    vfirst_out_ref[b, rows, :] = v[b * CHUNK:(b + 1) * CHUNK]
            store(0, (yn + bonus) * _silu(load(o_rwz, o_rwz + rw_w)))

        def hgrn_stages():
            q = load(o_hgq, o_hgq + hg_w)
            f = lb + (1.0 - lb) * _sigmoid(load(o_hgf, o_hgf + hg_w))
            g = jnp.log(f)
            kx = 1.0 - f
            iv = load(o_hgi, o_hgi + hg_w)
            bb = _dot(hgm_ref[...], jnp.concatenate(_split2(g), axis=0))
            part = lambda n: bb[n * rows_all:(n + 1) * rows_all]
            bsub = part(0)
            n_lv = len(HG_LEVELS)
            yield
            bchunk = part(n_lv + 1)
            qc = (q * jnp.exp(bchunk)).astype(BF16)
            kc = (kx * jnp.exp(part(n_lv + 2))).astype(BF16)
            ivb = iv.astype(BF16)
            ocs = {}
            for n, i in enumerate(insts):
                b, p = i
                hst = hgstate_ref[n]
                ocs[i] = _dot_nt(blk(qc, *i), hst.astype(BF16))
                upd = jnp.where(bd_state, _dot_tn(blk(ivb, *i), blk(kc, *i)), 0.0)
                last = (b + 1) * CHUNK - 1
                hgstate_ref[n] = hst * jnp.exp(bchunk[last:last + 1, p * LANES:(p + 1) * LANES]) + upd
            yield
            scs = {i: None for i in insts}
            for lv, size in enumerate(HG_LEVELS):
                pst = jnp.exp(part(1 + lv))
                upper = (row & (size - 1)) >= size // 2
                qs = jnp.where(upper, q * pst, 0.0)
                ks = jnp.where(upper, 0.0, kx * pst)
                for i in insts:
                    sc = jnp.where(same_group[size], _dot_nt(_stack_heads(blk(qs, *i), lane_head0).astype(BF16),
                                                             _stack_heads(blk(ks, *i), lane_head0).astype(BF16)), 0.0)
                    scs[i] = sc if scs[i] is None else scs[i] + sc
                yield
            ostr = {}
            for i in insts:
                ov = _mm(scs[i], _stack_heads(blk(iv, *i), lane_head0))
                ostr[i] = ov[0:CHUNK] + ov[CHUNK:c2] + ocs[i]
            yield
            sub = (rows_all // HG_SUB, HG_SUB, hg_w)
            pos8 = _iota((1, HG_SUB, 1), 1)
            q3, b3, k3, v3 = (z.reshape(sub) for z in (q, bsub, kx, iv))
            acc = segsum(q * kx) * iv
            for d in range(1, HG_SUB):
                wgt = jnp.where(pos8 >= d, q3 * jnp.exp(b3 - pltpu.roll(b3, d, axis=1)) * pltpu.roll(k3, d, axis=1), 0.0)
                acc = acc + segsum(wgt.reshape(rows_all, hg_w)) * pltpu.roll(v3, d, axis=1).reshape(rows_all, hg_w)
                if d % 2 == 1:
                    yield
            o = acc + jnp.concatenate(
                [jnp.concatenate([ostr[(b, p)] for p in range(n_pairs)], axis=1) for b in range(nb)], axis=0)
            ms = segsum(o * o) * (1.0 / HEAD)
            store(rw_w, o * lax.rsqrt(ms + NORM_EPS) * hgnw_ref[...] * _silu(load(o_hgz, o_hgz + hg_w)))

        def lru_stages():
            xb = load(o_lrx, o_lrx + lru_w)
            row8 = _iota((8, 1), 0)
            yv = convb_ref[...] + convw_ref[CONV_WIDTH - 1:CONV_WIDTH, :] * xb
            for d in range(1, CONV_WIDTH):
                rolled = pltpu.roll(xb, d, axis=0)
                pieces = []
                for b in range(nb):
                    tail = pltpu.roll(lrux_ref[b], d, axis=0)
                    pieces.append(jnp.where(row8 < d, tail, rolled[b * CHUNK:b * CHUNK + 8]))
                    pieces.append(rolled[b * CHUNK + 8:(b + 1) * CHUNK])
                yv = yv + convw_ref[CONV_WIDTH - 1 - d:CONV_WIDTH - d, :] * jnp.concatenate(pieces, axis=0)
            for b in range(nb):
                lrux_ref[b] = xb[(b + 1) * CHUNK - 8:(b + 1) * CHUNK]
            yield
            ybf = yv.astype(BF16)
            rg = _sigmoid(_dot(ybf, wa_ref[...]) + ba_ref[...])
            ig = _sigmoid(_dot(ybf, wx_ref[...]) + bx_ref[...])
            log_a = -LRU_C * rg * _softplus(-lam_ref[...])
            av = jnp.exp(log_a)
            th = jnp.tanh(log_a)
            gu = jnp.sqrt(-2.0 * th / (1.0 - th)) * (ig * yv)
            yield
            step = 1
            while step < CHUNK:
                keep = tpos >= step
                a_sh = jnp.where(keep, pltpu.roll(av, step, axis=0), 1.0)
                u_sh = jnp.where(keep, pltpu.roll(gu, step, axis=0), 0.0)
                gu = av * u_sh + gu
                av = av * a_sh
                step *= 2
                yield
            hv = gu + av * per_batch(lruh_ref)
            for b in range(nb):
                lruh_ref[b] = hv[(b + 1) * CHUNK - 1:(b + 1) * CHUNK]
            store(rw_w + hg_w, hv * _silu(load(o_lrz, o_lrz + lru_w)))

        pending = [rwkv_stages(), hgrn_stages(), lru_stages()]
        while pending:
            for gen in list(pending):
                if next(gen, True):
                    pending.remove(gen)
        return carry

    lax.fori_loop(0, n_chunks, chunk_body, 0)


def _mixer_constants(nb):
    rows_all = nb * CHUNK
    t = np.arange(rows_all)
    same_chunk = (t[None, :] // CHUNK) == (t[:, None] // CHUNK)
    lower = t[None, :] <= t[:, None]
    tri2 = np.tile(lower & same_chunk, (1, 2))
    mats = [lower & ((t[None, :] // HG_SUB) == (t[:, None] // HG_SUB))]
    for size in HG_LEVELS:
        half = size // 2
        same = (t[None, :] // size) == (t[:, None] // size)
        upper_row = (t[:, None] % size) >= half
        upper_col = (t[None, :] % size) >= half
        mats.append(same & np.where(upper_row, upper_col & lower, (~upper_col) & (~lower)))
    mats.append(lower & same_chunk)
    mats.append((~lower) & same_chunk)
    hgm = np.tile(np.concatenate(mats, axis=0), (1, 2))
    h = np.arange(2 * LANES) // HEAD
    seg256 = h[None, :] == h[:, None]
    as_bf16 = lambda m: jnp.asarray(m.astype(np.float32), dtype=BF16)
    return as_bf16(tri2), as_bf16(hgm), as_bf16(seg256)


def _block_diag(w):
    g, n, _ = w.shape
    out = jnp.zeros((g * n, g * n), w.dtype)
    for i in range(g):
        out = out.at[i * n:(i + 1) * n, i * n:(i + 1) * n].set(w[i])
    return out


def _mixer(u, v_first, layer, p, batch, seq):
    n_in = u.shape[1]
    rw_w = p["w0"].shape[1]
    hg_w = p["hgnw"].shape[1]
    lru_w = p["lam"].shape[1]
    d_mix = rw_w + hg_w + lru_w
    tb = _row_block(seq, 512)
    assert tb % CHUNK == 0 and rw_w == 3 * LANES and hg_w == rw_w
    has_vres = v_first is not None

    def tok_spec(width):
        return pl.BlockSpec((batch, tb, width), lambda t: (0, t, 0))

    def full_spec(a):
        nd = a.ndim
        return pl.BlockSpec(a.shape, lambda t: (0,) * nd)

    args, in_specs = [u.reshape(batch, seq, n_in)], [tok_spec(n_in)]
    if has_vres:
        args.append(v_first)
        in_specs.append(tok_spec(rw_w))
    names = ["mu", "w0", "wup", "a0", "aup"]
    if has_vres:
        names += ["v0", "vdn", "vup"]
    names += ["kk", "ka", "rk", "lnw", "lnb", "lbraw", "hgnw",
              "convw", "convb", "wa", "ba", "wx", "bx", "lam"]
    for a in [p[k] for k in names] + list(_mixer_constants(batch)):
        args.append(a)
        in_specs.append(full_spec(a))

    out_shape = [jax.ShapeDtypeStruct((batch, seq, d_mix), BF16)]
    out_specs = [tok_spec(d_mix)]
    if not has_vres:
        out_shape.append(jax.ShapeDtypeStruct((batch, seq, rw_w), F32))
        out_specs.append(tok_spec(rw_w))

    n_pairs = rw_w // LANES
    scratch = [pltpu.VMEM((batch, 1, 3 * rw_w + LANES), F32),
               pltpu.VMEM((batch * n_pairs, LANES, LANES), F32),
               pltpu.VMEM((batch * n_pairs, LANES, LANES), F32),
               pltpu.VMEM((batch, 8, lru_w), F32),
               pltpu.VMEM((batch, 1, lru_w), F32)]
    block_bytes = batch * tb * (n_in * 4 + d_mix * 2 + 2 * rw_w * 4)
    body = functools.partial(_mix_kernel, layer=layer, has_vres=has_vres, n_chunks=tb // CHUNK, nb=batch,
                             rw_w=rw_w, hg_w=hg_w, lru_w=lru_w)
    res = pl.pallas_call(
        body,
        out_shape=out_shape,
        grid=(seq // tb,),
        in_specs=in_specs,
        out_specs=out_specs,
        scratch_shapes=scratch,
        compiler_params=pltpu.CompilerParams(
            dimension_semantics=("arbitrary",), vmem_limit_bytes=_vmem_limit(block_bytes)),
        name=f"mixer_l{layer}",
    )(*args)
    mix = res[0].reshape(batch * seq, d_mix)
    if has_vres:
        return mix, v_first
    return mix, res[1]


def kernel(x, norm_w, w_in, rw_mu, rw_w0, rw_w_up, rw_a0, rw_a_up, rw_v0, rw_v_dn, rw_v_up, rw_k_k, rw_k_a, rw_r_k, rw_ln_w, rw_ln_b, hg_lb_raw, hg_norm_w, lru_conv_w, lru_conv_b, lru_wa, lru_ba, lru_wx, lru_bx, lru_lambda, w_out, final_norm_w):
    batch, seq, d = x.shape
    depth = w_in.shape[0]
    rw_w = rw_w0.shape[1]
    lora = rw_w_up.shape[1]
    assert 2 * lora == LANES
    x2 = x.reshape(batch * seq, d)
    zeros_code = jnp.zeros((lora, rw_w), F32)
    row = lambda a: a.reshape(1, -1)
    v_first = None
    for l in range(depth):
        p = {
            "mu": row(rw_mu[l]), "w0": row(rw_w0[l]),
            "wup": jnp.concatenate([rw_w_up[l], zeros_code], axis=0).astype(BF16),
            "a0": row(rw_a0[l]),
            "aup": jnp.concatenate([zeros_code, rw_a_up[l]], axis=0).astype(BF16),
            "kk": row(rw_k_k[l]), "ka": row(rw_k_a[l]), "rk": row(rw_r_k[l]),
            "lnw": row(rw_ln_w[l]), "lnb": row(rw_ln_b[l]),
            "lbraw": hg_lb_raw, "hgnw": row(hg_norm_w[l]),
            "convw": lru_conv_w[l], "convb": row(lru_conv_b[l]),
            "wa": _block_diag(lru_wa[l]).astype(BF16), "ba": row(lru_ba[l]),
            "wx": _block_diag(lru_wx[l]).astype(BF16), "bx": row(lru_bx[l]),
            "lam": row(lru_lambda[l]),
        }
        if l > 0:
            p["v0"] = row(rw_v0[l - 1])
            p["vdn"] = rw_v_dn[l - 1].astype(BF16)
            p["vup"] = rw_v_up[l - 1].astype(BF16)
        u = _inproj(x2, row(norm_w[l]), w_in[l].astype(BF16))
        mix, v_first = _mixer(u, v_first, l, p, batch, seq)
        fw = row(final_norm_w) if l == depth - 1 else None
        x2 = _outproj(x2, mix, w_out[l].astype(BF16), fw)
    return x2.reshape(batch, seq, d)
```

```python
import functools

import numpy as np
import jax
import jax.numpy as jnp
from jax import lax
from jax.experimental import pallas as pl
from jax.experimental.pallas import tpu as pltpu

F32 = jnp.float32
BF16 = jnp.bfloat16

NORM_EPS = 1e-6
RW_GN_EPS = 64e-5
LRU_C = 8.0
HEAD = 64
LANES = 128
SUBLANES = 8
CHUNK = 64
HG_SUB = 4
HG_LEVELS = (8, 16, 32, 64)
CONV_WIDTH = 4
LOG2_HEAD = HEAD.bit_length() - 1
LOG2_CHUNK = CHUNK.bit_length() - 1
LOG2_E = 1.4426950408889634
V7X_VMEM_BYTES = 64 * 1024 * 1024


def _dot(a, b):
    return jnp.dot(a, b, preferred_element_type=F32)


def _dot_nt(a, b):
    return lax.dot_general(a, b, (((1,), (1,)), ((), ())), preferred_element_type=F32)


def _dot_tn(a, b):
    return lax.dot_general(a, b, (((0,), (0,)), ((), ())), preferred_element_type=F32)


def _mm(a, b):
    return _dot(a.astype(BF16), b.astype(BF16))


def _split2(x):
    hi = x.astype(BF16)
    lo = (x - hi.astype(F32)).astype(BF16)
    return hi, lo


def _sigmoid(x):
    return 1.0 / (1.0 + jnp.exp(-x))


def _silu(x):
    return x * _sigmoid(x)


def _softplus(x):
    return jnp.maximum(x, 0.0) + jnp.log1p(jnp.exp(-jnp.abs(x)))


def _iota(shape, dim):
    return lax.broadcasted_iota(jnp.int32, shape, dim)


def _head_sums(x, seg256):
    xb = x.astype(BF16)
    seg128 = seg256[0:LANES, 0:LANES]
    return jnp.concatenate([_dot(xb[:, 0:2 * LANES], seg256), _dot(xb[:, 2 * LANES:3 * LANES], seg128)], axis=1)


def _inproj_kernel(*refs, has_vres, rw_w, tiles_per_seq):
    it = iter(refs)
    x_ref, nw_ref, w_ref = (next(it) for _ in range(3))
    vfirst_in_ref = next(it) if has_vres else None
    mu_ref, w0_ref, a0_ref, wcode_ref = (next(it) for _ in range(4))
    if has_vres:
        v0_ref, vdn_ref, vup_ref = (next(it) for _ in range(3))
    kk_ref, ka_ref, rk_ref, tri2_ref, seg256_ref = (next(it) for _ in range(5))
    pack_ref, v_ref, bonus_ref, elast_ref, rest_ref = (next(it) for _ in range(5))
    vfirst_out_ref = None if has_vres else next(it)
    prev_ref = next(it)

    tm = x_ref.shape[0]
    sub = 2 * CHUNK
    n_sub = tm // sub
    rw_shift = 3 * rw_w + LANES
    rest_tiles = (w_ref.shape[1] - rw_shift) // LANES

    @pl.when(lax.rem(pl.program_id(0), tiles_per_seq) == 0)
    def _new_sequence():
        prev_ref[...] = jnp.zeros_like(prev_ref)

    x = x_ref[...]
    ms = jnp.mean(x * x, axis=-1, keepdims=True)
    hb = (x * lax.rsqrt(ms + NORM_EPS) * nw_ref[...]).astype(BF16)
    u_rw = _dot(hb, w_ref[:, 0:rw_shift])

    seg256 = seg256_ref[...]
    row = _iota((sub, 1), 0)
    lane_first_half = _iota((1, LANES), 1) < LANES // 2
    prev = prev_ref[...]
    for s in range(n_sub):
        lo = rw_shift + (rest_tiles * s // n_sub) * LANES
        hi = rw_shift + (rest_tiles * (s + 1) // n_sub) * LANES
        rest_ref[:, lo - rw_shift:hi - rw_shift] = _dot(hb, w_ref[:, lo:hi])

        rs = slice(s * sub, (s + 1) * sub)
        ru = u_rw[rs]
        sh = jnp.where(row == 0, prev, pltpu.roll(ru, 1, axis=0))
        prev = ru[sub - 1:sub, :]
        ul = ru + mu_ref[...] * (sh - ru)
        r = ul[:, 0:rw_w]
        k = ul[:, rw_w:2 * rw_w]
        v = ul[:, 2 * rw_w:3 * rw_w]
        codes = ul[:, 3 * rw_w:rw_shift]
        both = _mm(jnp.where(lane_first_half, jnp.tanh(codes), codes), wcode_ref[...])
        w_log = -_softplus(-(w0_ref[...] + both[:, 0:rw_w])) - 0.5
        nlw2 = jnp.exp(w_log) * LOG2_E
        a = _sigmoid(a0_ref[...] + both[:, rw_w:2 * rw_w])
        if has_vres:
            gate = _sigmoid(v0_ref[...] + _mm(_mm(v, vdn_ref[...]), vup_ref[...]))
            v = v + (vfirst_in_ref[rs, :] - v) * gate
        else:
            vfirst_out_ref[rs, :] = v
        kk = k * kk_ref[...]
        kk = kk / jnp.maximum(jnp.sqrt(_head_sums(kk * kk, seg256)), 1e-12)
        k2 = k * (1.0 + (a - 1.0) * ka_ref[...])
        bonus_ref[rs, :] = _head_sums(r * k2 * rk_ref[...], seg256) * v
        v_ref[rs, :] = v.astype(BF16)
        cs = _dot(tri2_ref[...], jnp.concatenate(_split2(nlw2), axis=0))
        e_pos = jnp.exp2(-cs)
        e_neg = jnp.exp2(cs)
        pack_ref[rs, 0:rw_w] = (-kk * jnp.exp2(nlw2 - cs)).astype(BF16)
        pack_ref[rs, rw_w:2 * rw_w] = (r * e_pos).astype(BF16)
        pack_ref[rs, 2 * rw_w:3 * rw_w] = (kk * a * e_neg).astype(BF16)
        pack_ref[rs, 3 * rw_w:4 * rw_w] = (k2 * e_neg).astype(BF16)
        for c in range(sub // CHUNK):
            n = s * (sub // CHUNK) + c
            elast_ref[n:n + 1, :] = e_pos[(c + 1) * CHUNK - 1:(c + 1) * CHUNK, :]
    prev_ref[...] = prev


def _outproj_kernel(x_ref, m_ref, w_ref, o_ref):
    o_ref[...] = x_ref[...] + _dot(m_ref[...], w_ref[...])


def _outproj_final_kernel(x_ref, m_ref, w_ref, fw_ref, o_ref):
    x = x_ref[...] + _dot(m_ref[...], w_ref[...])
    ms = jnp.mean(x * x, axis=-1, keepdims=True)
    o_ref[...] = x * lax.rsqrt(ms + NORM_EPS) * fw_ref[...]


def _row_block(n_rows, target):
    blk = min(target, n_rows)
    while n_rows % blk:
        blk //= 2
    return blk


def _vmem_limit(block_bytes):
    return int(min(V7X_VMEM_BYTES * 7 // 8, 2 * block_bytes + 16 * 1024 * 1024))


def _inproj(x2, norm_w, w_bf16, v_first, p, seq):
    n, d = x2.shape
    n_in = w_bf16.shape[1]
    rw_w = p["w0"].shape[1]
    rw_shift = 3 * rw_w + LANES
    tm = _row_block(seq, 512)
    assert tm % (SUBLANES * CHUNK) == 0 and seq % tm == 0
    has_vres = v_first is not None

    def tok_spec(width):
        return pl.BlockSpec((tm, width), lambda i: (i, 0))

    def full_spec(a):
        nd = a.ndim
        return pl.BlockSpec(a.shape, lambda i: (0,) * nd)

    tri2, _, seg256 = _mixer_constants(2)
    args, in_specs = [x2, norm_w, w_bf16], [tok_spec(d), full_spec(norm_w), full_spec(w_bf16)]
    if has_vres:
        args.append(v_first)
        in_specs.append(tok_spec(rw_w))
    names = ["mu", "w0", "a0", "wcode"] + (["v0", "vdn", "vup"] if has_vres else []) + ["kk", "ka", "rk"]
    for a in [p[k] for k in names] + [tri2, seg256]:
        args.append(a)
        in_specs.append(full_spec(a))
    out_shape = [jax.ShapeDtypeStruct((n, 4 * rw_w), BF16), jax.ShapeDtypeStruct((n, rw_w), BF16),
                 jax.ShapeDtypeStruct((n, rw_w), F32), jax.ShapeDtypeStruct((n // CHUNK, rw_w), F32),
                 jax.ShapeDtypeStruct((n, n_in - rw_shift), F32)]
    out_specs = [tok_spec(4 * rw_w), tok_spec(rw_w), tok_spec(rw_w),
                 pl.BlockSpec((tm // CHUNK, rw_w), lambda i: (i, 0)), tok_spec(n_in - rw_shift)]
    if not has_vres:
        out_shape.append(jax.ShapeDtypeStruct((n, rw_w), F32))
        out_specs.append(tok_spec(rw_w))
    block_bytes = tm * d * 4 + d * n_in * 2 + tm * (n_in + 4 * rw_w) * 4
    body = functools.partial(_inproj_kernel, has_vres=has_vres, rw_w=rw_w, tiles_per_seq=seq // tm)
    return pl.pallas_call(
        body,
        out_shape=out_shape,
        grid=(n // tm,),
        in_specs=in_specs,
        out_specs=out_specs,
        scratch_shapes=[pltpu.VMEM((1, rw_shift), F32)],
        compiler_params=pltpu.CompilerParams(
            dimension_semantics=("arbitrary",), vmem_limit_bytes=_vmem_limit(block_bytes)),
        name="inproj_vres" if has_vres else "inproj",
    )(*args)


def _outproj(x2, mix, w_bf16, final_w=None):
    n, d = x2.shape
    dm = mix.shape[1]
    tm = _row_block(n, 512)
    block_bytes = 2 * tm * d * 4 + tm * dm * 2 + dm * d * 2
    in_specs = [pl.BlockSpec((tm, d), lambda i: (i, 0)),
                pl.BlockSpec((tm, dm), lambda i: (i, 0)),
                pl.BlockSpec((dm, d), lambda i: (0, 0))]
    args = [x2, mix, w_bf16]
    body = _outproj_kernel
    if final_w is not None:
        in_specs.append(pl.BlockSpec((1, d), lambda i: (0, 0)))
        args.append(final_w)
        body = _outproj_final_kernel
    return pl.pallas_call(
        body,
        out_shape=jax.ShapeDtypeStruct((n, d), F32),
        grid=(n // tm,),
        in_specs=in_specs,
        out_specs=pl.BlockSpec((tm, d), lambda i: (i, 0)),
        compiler_params=pltpu.CompilerParams(
            dimension_semantics=("arbitrary",), vmem_limit_bytes=_vmem_limit(block_bytes)),
        name="outproj_final" if final_w is not None else "outproj",
    )(*args)


def _stack_heads(x, lane_head0):
    return jnp.concatenate([jnp.where(lane_head0, x, 0.0), jnp.where(lane_head0, 0.0, x)], axis=0)


def _mix_kernel(*refs, layer, n_chunks, nb, rw_w, hg_w, lru_w):
    it = iter(refs)
    pack_ref, v_ref, bonus_ref, elast_ref, u_ref = (next(it) for _ in range(5))
    lnw_ref, lnb_ref, lbraw_ref, hgnw_ref = (next(it) for _ in range(4))
    convw_ref, convb_ref, wa_ref, ba_ref, wx_ref, bx_ref, lam_ref = (next(it) for _ in range(7))
    hgm_ref, seg256_ref = (next(it) for _ in range(2))
    out_ref = next(it)
    rwstate_ref, hgstate_ref, lrux_ref, lruh_ref = (next(it) for _ in range(4))

    n_pairs = rw_w // LANES
    rows_all = nb * CHUNK
    c2 = 2 * CHUNK
    o_rwz = 0
    o_hgq = o_rwz + rw_w
    o_hgf = o_hgq + hg_w
    o_hgi = o_hgf + hg_w
    o_hgz = o_hgi + hg_w
    o_lrx = o_hgz + hg_w
    o_lrz = o_lrx + lru_w

    @pl.when(pl.program_id(0) == 0)
    def _reset():
        rwstate_ref[...] = jnp.zeros_like(rwstate_ref)
        hgstate_ref[...] = jnp.zeros_like(hgstate_ref)
        lrux_ref[...] = jnp.zeros_like(lrux_ref)
        lruh_ref[...] = jnp.zeros_like(lruh_ref)

    row = _iota((rows_all, 1), 0)
    tpos = row & (CHUNK - 1)
    lane_head0 = _iota((1, LANES), 1) < HEAD
    si = _iota((c2, c2), 0)
    sj = _iota((c2, c2), 1)
    same_head = (si >> LOG2_CHUNK) == (sj >> LOG2_CHUNK)
    ti = si & (CHUNK - 1)
    tj = sj & (CHUNK - 1)
    strict_lower = same_head & (tj < ti)
    incl_lower = same_head & (tj <= ti)
    eye = jnp.where(si == sj, 1.0, 0.0)
    same_group = {size: (si >> (size.bit_length() - 1)) == (sj >> (size.bit_length() - 1)) for size in HG_LEVELS}
    bd_state = (_iota((LANES, LANES), 0) >> LOG2_HEAD) == (_iota((LANES, LANES), 1) >> LOG2_HEAD)
    insts = [(b, p) for p in range(n_pairs) for b in range(nb)]

    seg256 = seg256_ref[...]

    def segsum(x):
        return _head_sums(x, seg256)

    def per_batch(ref):
        val = ref[0]
        for b in range(1, nb):
            val = jnp.where(row >= b * CHUNK, ref[b], val)
        return val

    lbraw = lbraw_ref[...]
    lbe = jnp.exp(lbraw - jnp.max(lbraw, axis=0, keepdims=True))
    lbw = lbe / jnp.sum(lbe, axis=0, keepdims=True)
    lb = lbw[0:1, :]
    for j in range(1, layer + 1):
        lb = lb + lbw[j:j + 1, :]
    lb = lb - lbw[0:1, :]

    def blk(x, b, p):
        return x[b * CHUNK:(b + 1) * CHUNK, p * LANES:(p + 1) * LANES]

    def chunk_body(c, carry):
        r0 = pl.multiple_of(c * CHUNK, CHUNK)
        rows = pl.ds(r0, CHUNK)

        def tokens(ref, lo, hi):
            return jnp.concatenate([ref[b, rows, lo:hi] for b in range(nb)], axis=0)

        def load(lo, hi):
            return tokens(u_ref, lo, hi)

        def store(lo, val):
            for b in range(nb):
                out_ref[b, rows, lo:lo + val.shape[1]] = val[b * CHUNK:(b + 1) * CHUNK].astype(out_ref.dtype)

        def rwkv_stages():
            a_t = tokens(pack_ref, 0, rw_w)
            r_t = tokens(pack_ref, rw_w, 2 * rw_w)
            b_t = tokens(pack_ref, 2 * rw_w, 3 * rw_w)
            k_t = tokens(pack_ref, 3 * rw_w, 4 * rw_w)
            v = tokens(v_ref, 0, rw_w)
            lhs, rhs, vs, a_ak, lrk, avs, pw, tinv = {}, {}, {}, {}, {}, {}, {}, {}
            for i in insts:
                lhs[i] = jnp.concatenate([_stack_heads(blk(a_t, *i), lane_head0),
                                          _stack_heads(blk(r_t, *i), lane_head0)], axis=0)
                rhs[i] = jnp.concatenate([_stack_heads(blk(b_t, *i), lane_head0),
                                          _stack_heads(blk(k_t, *i), lane_head0)], axis=0)
                vs[i] = _stack_heads(blk(v, *i), lane_head0)
            yield
            for i in insts:
                sc = _dot_nt(lhs[i], rhs[i])
                pw[i] = jnp.where(strict_lower, sc[0:c2, 0:c2], 0.0)
                a_ak[i] = jnp.where(strict_lower, sc[0:c2, c2:2 * c2], 0.0).astype(BF16)
                lrk[i] = jnp.concatenate([jnp.where(incl_lower, sc[c2:2 * c2, 0:c2], 0.0),
                                          jnp.where(incl_lower, sc[c2:2 * c2, c2:2 * c2], 0.0)], axis=1).astype(BF16)
            yield
            for i in insts:
                avs[i] = _dot(a_ak[i], vs[i])
                tinv[i] = eye + pw[i]
                pw[i] = _mm(pw[i], pw[i])
            yield
            for s in range(1, 5):
                for i in insts:
                    both = _mm(jnp.concatenate([pw[i], tinv[i]], axis=0), pw[i])
                    pw[i] = both[0:c2]
                    tinv[i] = tinv[i] + both[c2:2 * c2]
                yield
            for i in insts:
                tinv[i] = (tinv[i] + _mm(tinv[i], pw[i])).astype(BF16)
            yield
            st, am, uv, yb = {}, {}, {}, {}
            for n, i in enumerate(insts):
                st[i] = rwstate_ref[n]
                am[i] = _dot_nt(lhs[i], st[i].astype(BF16))
            yield
            for i in insts:
                x = am[i][0:c2] + avs[i]
                uv[i] = jnp.concatenate([_dot(tinv[i], x.astype(BF16)).astype(BF16), vs[i]], axis=0)
            yield
            for i in insts:
                yy = am[i][c2:2 * c2] + _dot(lrk[i], uv[i])
                yb[i] = yy[0:CHUNK] + yy[CHUNK:c2]
            yield
            for n, i in enumerate(insts):
                b, p = i
                rwstate_ref[n] = (st[i] + _dot_tn(uv[i], rhs[i])) * elast_ref[b, c][:, p * LANES:(p + 1) * LANES]
            yield
            y = jnp.concatenate([jnp.concatenate([yb[(b, p)] for p in range(n_pairs)], axis=1) for b in range(nb)], axis=0)
            mean = segsum(y) * (1.0 / HEAD)
            yc = y - mean
            yield
            var = segsum(yc * yc) * (1.0 / HEAD)
            yn = yc * lax.rsqrt(var + RW_GN_EPS) * lnw_ref[...] + lnb_ref[...]
            store(0, (yn + tokens(bonus_ref, 0, rw_w)) * _silu(load(o_rwz, o_rwz + rw_w)))

        def hgrn_stages():
            q = load(o_hgq, o_hgq + hg_w)
            f = lb + (1.0 - lb) * _sigmoid(load(o_hgf, o_hgf + hg_w))
            g = jnp.log2(f)
            kx = 1.0 - f
            iv = load(o_hgi, o_hgi + hg_w)
            bb = _dot(hgm_ref[...], jnp.concatenate(_split2(g), axis=0))
            part = lambda n: bb[n * rows_all:(n + 1) * rows_all]
            bsub = part(0)
            n_lv = len(HG_LEVELS)
            yield
            bchunk = part(n_lv + 1)
            qc = (q * jnp.exp2(bchunk)).astype(BF16)
            kc = (kx * jnp.exp2(part(n_lv + 2))).astype(BF16)
            ivb = iv.astype(BF16)
            ocs = {}
            for n, i in enumerate(insts):
                b, p = i
                hst = hgstate_ref[n]
                ocs[i] = _dot_nt(blk(qc, *i), hst.astype(BF16))
                upd = jnp.where(bd_state, _dot_tn(blk(ivb, *i), blk(kc, *i)), 0.0)
                last = (b + 1) * CHUNK - 1
                hgstate_ref[n] = hst * jnp.exp2(bchunk[last:last + 1, p * LANES:(p + 1) * LANES]) + upd
            yield
            scs = {i: None for i in insts}
            for lv, size in enumerate(HG_LEVELS):
                pst = jnp.exp2(part(1 + lv))
                upper = (row & (size - 1)) >= size // 2
                qs = jnp.where(upper, q * pst, 0.0)
                ks = jnp.where(upper, 0.0, kx * pst)
                for i in insts:
                    sc = jnp.where(same_group[size], _dot_nt(_stack_heads(blk(qs, *i), lane_head0).astype(BF16),
                                                             _stack_heads(blk(ks, *i), lane_head0).astype(BF16)), 0.0)
                    scs[i] = sc if scs[i] is None else scs[i] + sc
                yield
            ostr = {}
            for i in insts:
                ov = _mm(scs[i], _stack_heads(blk(iv, *i), lane_head0))
                ostr[i] = ov[0:CHUNK] + ov[CHUNK:c2] + ocs[i]
            yield
            sub = (rows_all // SUBLANES, SUBLANES, hg_w)
            pos = _iota((1, SUBLANES, 1), 1) & (HG_SUB - 1)
            q3, b3, k3, v3 = (z.reshape(sub) for z in (q, bsub, kx, iv))
            acc = segsum(q * kx) * iv
            for d in range(1, HG_SUB):
                wgt = jnp.where(pos >= d, q3 * jnp.exp2(b3 - pltpu.roll(b3, d, axis=1)) * pltpu.roll(k3, d, axis=1), 0.0)
                acc = acc + segsum(wgt.reshape(rows_all, hg_w)) * pltpu.roll(v3, d, axis=1).reshape(rows_all, hg_w)
                if d % 2 == 1:
                    yield
            o = acc + jnp.concatenate(
                [jnp.concatenate([ostr[(b, p)] for p in range(n_pairs)], axis=1) for b in range(nb)], axis=0)
            ms = segsum(o * o) * (1.0 / HEAD)
            store(rw_w, o * lax.rsqrt(ms + NORM_EPS) * hgnw_ref[...] * _silu(load(o_hgz, o_hgz + hg_w)))

        def lru_stages():
            xb = load(o_lrx, o_lrx + lru_w)
            row8 = _iota((SUBLANES, 1), 0)
            yv = convb_ref[...] + convw_ref[CONV_WIDTH - 1:CONV_WIDTH, :] * xb
            for d in range(1, CONV_WIDTH):
                rolled = pltpu.roll(xb, d, axis=0)
                pieces = []
                for b in range(nb):
                    tail = pltpu.roll(lrux_ref[b], d, axis=0)
                    pieces.append(jnp.where(row8 < d, tail, rolled[b * CHUNK:b * CHUNK + SUBLANES]))
                    pieces.append(rolled[b * CHUNK + SUBLANES:(b + 1) * CHUNK])
                yv = yv + convw_ref[CONV_WIDTH - 1 - d:CONV_WIDTH - d, :] * jnp.concatenate(pieces, axis=0)
            for b in range(nb):
                lrux_ref[b] = xb[(b + 1) * CHUNK - SUBLANES:(b + 1) * CHUNK]
            yield
            ybf = yv.astype(BF16)
            rg = _sigmoid(_dot(ybf, wa_ref[...]) + ba_ref[...])
            ig = _sigmoid(_dot(ybf, wx_ref[...]) + bx_ref[...])
            log_a = -LRU_C * rg * _softplus(-lam_ref[...])
            av = jnp.exp(log_a)
            th = jnp.tanh(log_a)
            gu = jnp.sqrt(-2.0 * th / (1.0 - th)) * (ig * yv)
            yield
            step = 1
            while step < CHUNK:
                keep = tpos >= step
                a_sh = jnp.where(keep, pltpu.roll(av, step, axis=0), 1.0)
                u_sh = jnp.where(keep, pltpu.roll(gu, step, axis=0), 0.0)
                gu = av * u_sh + gu
                av = av * a_sh
                step *= 2
                yield
            hv = gu + av * per_batch(lruh_ref)
            for b in range(nb):
                lruh_ref[b] = hv[(b + 1) * CHUNK - 1:(b + 1) * CHUNK]
            store(rw_w + hg_w, hv * _silu(load(o_lrz, o_lrz + lru_w)))

        pending = [rwkv_stages(), hgrn_stages(), lru_stages()]
        while pending:
            for gen in list(pending):
                if next(gen, True):
                    pending.remove(gen)
        return carry

    lax.fori_loop(0, n_chunks, chunk_body, 0)


def _mixer_constants(nb):
    rows_all = nb * CHUNK
    t = np.arange(rows_all)
    same_chunk = (t[None, :] // CHUNK) == (t[:, None] // CHUNK)
    lower = t[None, :] <= t[:, None]
    tri2 = np.tile(lower & same_chunk, (1, 2))
    mats = [lower & ((t[None, :] // HG_SUB) == (t[:, None] // HG_SUB))]
    for size in HG_LEVELS:
        half = size // 2
        same = (t[None, :] // size) == (t[:, None] // size)
        upper_row = (t[:, None] % size) >= half
        upper_col = (t[None, :] % size) >= half
        mats.append(same & np.where(upper_row, upper_col & lower, (~upper_col) & (~lower)))
    mats.append(lower & same_chunk)
    mats.append((~lower) & same_chunk)
    hgm = np.tile(np.concatenate(mats, axis=0), (1, 2))
    h = np.arange(2 * LANES) // HEAD
    seg256 = h[None, :] == h[:, None]
    as_bf16 = lambda m: jnp.asarray(m.astype(np.float32), dtype=BF16)
    return as_bf16(tri2), as_bf16(hgm), as_bf16(seg256)


def _block_diag(w):
    g, n, _ = w.shape
    out = jnp.zeros((g * n, g * n), w.dtype)
    for i in range(g):
        out = out.at[i * n:(i + 1) * n, i * n:(i + 1) * n].set(w[i])
    return out


def _mixer(prep, layer, p, batch, seq):
    pack, vb, bonus, elast, rest = prep
    rw_w = vb.shape[1]
    hg_w = p["hgnw"].shape[1]
    lru_w = p["lam"].shape[1]
    d_mix = rw_w + hg_w + lru_w
    tb = _row_block(seq, 512)
    assert tb % CHUNK == 0 and rw_w == 3 * LANES and hg_w == rw_w

    def tok_spec(width):
        return pl.BlockSpec((batch, tb, width), lambda t: (0, t, 0))

    def full_spec(a):
        nd = a.ndim
        return pl.BlockSpec(a.shape, lambda t: (0,) * nd)

    per_token = lambda a: a.reshape(batch, seq, a.shape[1])
    args = [per_token(pack), per_token(vb), per_token(bonus), elast.reshape(batch, seq // CHUNK, 1, rw_w),
            per_token(rest)]
    in_specs = [tok_spec(4 * rw_w), tok_spec(rw_w), tok_spec(rw_w),
                pl.BlockSpec((batch, tb // CHUNK, 1, rw_w), lambda t: (0, t, 0, 0)), tok_spec(rest.shape[1])]
    names = ["lnw", "lnb", "lbraw", "hgnw", "convw", "convb", "wa", "ba", "wx", "bx", "lam"]
    _, hgm, seg256 = _mixer_constants(batch)
    for a in [p[k] for k in names] + [hgm, seg256]:
        args.append(a)
        in_specs.append(full_spec(a))

    n_pairs = rw_w // LANES
    scratch = [pltpu.VMEM((batch * n_pairs, LANES, LANES), F32),
               pltpu.VMEM((batch * n_pairs, LANES, LANES), F32),
               pltpu.VMEM((batch, SUBLANES, lru_w), F32),
               pltpu.VMEM((batch, 1, lru_w), F32)]
    block_bytes = batch * tb * (rest.shape[1] * 4 + 4 * rw_w * 2 + rw_w * 2 + rw_w * 4 + d_mix * 2)
    body = functools.partial(_mix_kernel, layer=layer, n_chunks=tb // CHUNK, nb=batch,
                             rw_w=rw_w, hg_w=hg_w, lru_w=lru_w)
    mix = pl.pallas_call(
        body,
        out_shape=jax.ShapeDtypeStruct((batch, seq, d_mix), BF16),
        grid=(seq // tb,),
        in_specs=in_specs,
        out_specs=tok_spec(d_mix),
        scratch_shapes=scratch,
        compiler_params=pltpu.CompilerParams(
            dimension_semantics=("arbitrary",), vmem_limit_bytes=_vmem_limit(block_bytes)),
        name=f"mixer_l{layer}",
    )(*args)
    return mix.reshape(batch * seq, d_mix)


def kernel(x, norm_w, w_in, rw_mu, rw_w0, rw_w_up, rw_a0, rw_a_up, rw_v0, rw_v_dn, rw_v_up, rw_k_k, rw_k_a, rw_r_k, rw_ln_w, rw_ln_b, hg_lb_raw, hg_norm_w, lru_conv_w, lru_conv_b, lru_wa, lru_ba, lru_wx, lru_bx, lru_lambda, w_out, final_norm_w):
    batch, seq, d = x.shape
    depth = w_in.shape[0]
    rw_w = rw_w0.shape[1]
    lora = rw_w_up.shape[1]
    assert 2 * lora == LANES
    x2 = x.reshape(batch * seq, d)
    zeros_code = jnp.zeros((lora, rw_w), F32)
    row = lambda a: a.reshape(1, -1)
    v_first = None
    for l in range(depth):
        p = {
            "mu": row(rw_mu[l]), "w0": row(rw_w0[l]), "a0": row(rw_a0[l]),
            "wcode": jnp.concatenate([jnp.concatenate([rw_w_up[l], zeros_code], axis=1),
                                      jnp.concatenate([zeros_code, rw_a_up[l]], axis=1)], axis=0).astype(BF16),
            "kk": row(rw_k_k[l]), "ka": row(rw_k_a[l]), "rk": row(rw_r_k[l]),
            "lnw": row(rw_ln_w[l]), "lnb": row(rw_ln_b[l]),
            "lbraw": hg_lb_raw, "hgnw": row(hg_norm_w[l]),
            "convw": lru_conv_w[l], "convb": row(lru_conv_b[l]),
            "wa": _block_diag(lru_wa[l]).astype(BF16), "ba": row(lru_ba[l]),
            "wx": _block_diag(lru_wx[l]).astype(BF16), "bx": row(lru_bx[l]),
            "lam": row(lru_lambda[l]),
        }
        if l > 0:
            p["v0"] = row(rw_v0[l - 1])
            p["vdn"] = rw_v_dn[l - 1].astype(BF16)
            p["vup"] = rw_v_up[l - 1].astype(BF16)
        prep = _inproj(x2, row(norm_w[l]), w_in[l].astype(BF16), v_first, p, seq)
        if l == 0:
            v_first = prep[5]
        mix = _mixer(prep[:5], l, p, batch, seq)
        fw = row(final_norm_w) if l == depth - 1 else None
        x2 = _outproj(x2, mix, w_out[l].astype(BF16), fw)
    return x2.reshape(batch, seq, d)
```

```python
import functools

import numpy as np
import jax
import jax.numpy as jnp
from jax import lax
from jax.experimental import pallas as pl
from jax.experimental.pallas import tpu as pltpu

F32 = jnp.float32
BF16 = jnp.bfloat16

NORM_EPS = 1e-6
RW_GN_EPS = 64e-5
LRU_C = 8.0
HEAD = 64
LANES = 128
SUBLANES = 8
CHUNK = 64
HG_SUB = 4
HG_LEVELS = (8, 16, 32, 64)
CONV_WIDTH = 4
LOG2_HEAD = HEAD.bit_length() - 1
LOG2_CHUNK = CHUNK.bit_length() - 1
LOG2_E = 1.4426950408889634
V7X_VMEM_BYTES = 64 * 1024 * 1024


def _dot(a, b):
    return jnp.dot(a, b, preferred_element_type=F32)


def _dot_nt(a, b):
    return lax.dot_general(a, b, (((1,), (1,)), ((), ())), preferred_element_type=F32)


def _dot_tn(a, b):
    return lax.dot_general(a, b, (((0,), (0,)), ((), ())), preferred_element_type=F32)


def _mm(a, b):
    return _dot(a.astype(BF16), b.astype(BF16))


def _split2(x):
    hi = x.astype(BF16)
    lo = (x - hi.astype(F32)).astype(BF16)
    return hi, lo


def _sigmoid(x):
    return 1.0 / (1.0 + jnp.exp(-x))


def _silu(x):
    return x * _sigmoid(x)


def _softplus(x):
    return jnp.maximum(x, 0.0) + jnp.log1p(jnp.exp(-jnp.abs(x)))


def _iota(shape, dim):
    return lax.broadcasted_iota(jnp.int32, shape, dim)


def _head_sums(x, seg256):
    xb = x.astype(BF16)
    seg128 = seg256[0:LANES, 0:LANES]
    return jnp.concatenate([_dot(xb[:, 0:2 * LANES], seg256), _dot(xb[:, 2 * LANES:3 * LANES], seg128)], axis=1)


def _inproj_kernel(*refs, has_vres, rw_w, tiles_per_seq):
    it = iter(refs)
    x_ref, nw_ref, w_ref = (next(it) for _ in range(3))
    vfirst_in_ref = next(it) if has_vres else None
    mu_ref, w0_ref, a0_ref, wcode_ref = (next(it) for _ in range(4))
    if has_vres:
        v0_ref, vdn_ref, vup_ref = (next(it) for _ in range(3))
    kk_ref, ka_ref, rk_ref, tri2_ref, seg256_ref = (next(it) for _ in range(5))
    pack_ref, v_ref, bonus_ref, elast_ref, rest_ref = (next(it) for _ in range(5))
    vfirst_out_ref = None if has_vres else next(it)
    prev_ref = next(it)

    tm = x_ref.shape[0]
    sub = 2 * CHUNK
    n_sub = tm // sub
    rw_shift = 3 * rw_w + LANES
    rest_tiles = (w_ref.shape[1] - rw_shift) // LANES

    @pl.when(lax.rem(pl.program_id(0), tiles_per_seq) == 0)
    def _new_sequence():
        prev_ref[...] = jnp.zeros_like(prev_ref)

    x = x_ref[...]
    ms = jnp.mean(x * x, axis=-1, keepdims=True)
    hb = (x * lax.rsqrt(ms + NORM_EPS) * nw_ref[...]).astype(BF16)
    u_rw = _dot(hb, w_ref[:, 0:rw_shift])

    seg256 = seg256_ref[...]
    row = _iota((sub, 1), 0)
    lane_first_half = _iota((1, LANES), 1) < LANES // 2
    prev = prev_ref[...]
    for s in range(n_sub):
        lo = rw_shift + (rest_tiles * s // n_sub) * LANES
        hi = rw_shift + (rest_tiles * (s + 1) // n_sub) * LANES
        rest_ref[:, lo - rw_shift:hi - rw_shift] = _dot(hb, w_ref[:, lo:hi])

        rs = slice(s * sub, (s + 1) * sub)
        ru = u_rw[rs]
        sh = jnp.where(row == 0, prev, pltpu.roll(ru, 1, axis=0))
        prev = ru[sub - 1:sub, :]
        ul = ru + mu_ref[...] * (sh - ru)
        r = ul[:, 0:rw_w]
        k = ul[:, rw_w:2 * rw_w]
        v = ul[:, 2 * rw_w:3 * rw_w]
        codes = ul[:, 3 * rw_w:rw_shift]
        both = _mm(jnp.where(lane_first_half, jnp.tanh(codes), codes), wcode_ref[...])
        w_log = -_softplus(-(w0_ref[...] + both[:, 0:rw_w])) - 0.5
        nlw2 = jnp.exp(w_log) * LOG2_E
        a = _sigmoid(a0_ref[...] + both[:, rw_w:2 * rw_w])
        if has_vres:
            gate = _sigmoid(v0_ref[...] + _mm(_mm(v, vdn_ref[...]), vup_ref[...]))
            v = v + (vfirst_in_ref[rs, :] - v) * gate
        else:
            vfirst_out_ref[rs, :] = v
        kk = k * kk_ref[...]
        kk = kk / jnp.maximum(jnp.sqrt(_head_sums(kk * kk, seg256)), 1e-12)
        k2 = k * (1.0 + (a - 1.0) * ka_ref[...])
        bonus_ref[rs, :] = _head_sums(r * k2 * rk_ref[...], seg256) * v
        v_ref[rs, :] = v.astype(BF16)
        cs = _dot(tri2_ref[...], jnp.concatenate(_split2(nlw2), axis=0))
        e_pos = jnp.exp2(-cs)
        e_neg = jnp.exp2(cs)
        pack_ref[rs, 0:rw_w] = (-kk * jnp.exp2(nlw2 - cs)).astype(BF16)
        pack_ref[rs, rw_w:2 * rw_w] = (r * e_pos).astype(BF16)
        pack_ref[rs, 2 * rw_w:3 * rw_w] = (kk * a * e_neg).astype(BF16)
        pack_ref[rs, 3 * rw_w:4 * rw_w] = (k2 * e_neg).astype(BF16)
        for c in range(sub // CHUNK):
            n = s * (sub // CHUNK) + c
            elast_ref[n:n + 1, :] = e_pos[(c + 1) * CHUNK - 1:(c + 1) * CHUNK, :]
    prev_ref[...] = prev


def _row_block(n_rows, target):
    blk = min(target, n_rows)
    while n_rows % blk:
        blk //= 2
    return blk


def _vmem_limit(block_bytes):
    return int(min(V7X_VMEM_BYTES * 7 // 8, 2 * block_bytes + 16 * 1024 * 1024))


def _layer_spec(a, idx):
    nd = a.ndim
    return pl.BlockSpec((None,) + a.shape[1:], lambda *_: (idx,) + (0,) * (nd - 1))


def _inproj(x2, norm_w, w_bf16, v_first, p, seq):
    n, d = x2.shape
    n_in = w_bf16[0].shape[-1]
    rw_w = p["w0"][0].shape[-1]
    rw_shift = 3 * rw_w + LANES
    tm = _row_block(seq, 512)
    assert tm % (SUBLANES * CHUNK) == 0 and seq % tm == 0
    has_vres = v_first is not None

    def tok_spec(width):
        return pl.BlockSpec((tm, width), lambda i: (i, 0))

    def full_spec(a):
        nd = a.ndim
        return pl.BlockSpec(a.shape, lambda i: (0,) * nd)

    tri2, _, seg256 = _mixer_constants(2)
    args, in_specs = [x2], [tok_spec(d)]
    for a, idx in (norm_w, w_bf16):
        args.append(a)
        in_specs.append(_layer_spec(a, idx))
    if has_vres:
        args.append(v_first)
        in_specs.append(tok_spec(rw_w))
    names = ["mu", "w0", "a0", "wcode"] + (["v0", "vdn", "vup"] if has_vres else []) + ["kk", "ka", "rk"]
    for a, idx in [p[k] for k in names]:
        args.append(a)
        in_specs.append(_layer_spec(a, idx))
    for a in (tri2, seg256):
        args.append(a)
        in_specs.append(full_spec(a))
    out_shape = [jax.ShapeDtypeStruct((n, 4 * rw_w), BF16), jax.ShapeDtypeStruct((n, rw_w), BF16),
                 jax.ShapeDtypeStruct((n, rw_w), F32), jax.ShapeDtypeStruct((n // CHUNK, rw_w), F32),
                 jax.ShapeDtypeStruct((n, n_in - rw_shift), F32)]
    out_specs = [tok_spec(4 * rw_w), tok_spec(rw_w), tok_spec(rw_w),
                 pl.BlockSpec((tm // CHUNK, rw_w), lambda i: (i, 0)), tok_spec(n_in - rw_shift)]
    if not has_vres:
        out_shape.append(jax.ShapeDtypeStruct((n, rw_w), F32))
        out_specs.append(tok_spec(rw_w))
    block_bytes = tm * d * 4 + d * n_in * 2 + tm * (n_in + 4 * rw_w) * 4
    body = functools.partial(_inproj_kernel, has_vres=has_vres, rw_w=rw_w, tiles_per_seq=seq // tm)
    return pl.pallas_call(
        body,
        out_shape=out_shape,
        grid=(n // tm,),
        in_specs=in_specs,
        out_specs=out_specs,
        scratch_shapes=[pltpu.VMEM((1, rw_shift), F32)],
        compiler_params=pltpu.CompilerParams(
            dimension_semantics=("arbitrary",), vmem_limit_bytes=_vmem_limit(block_bytes)),
        name="inproj_vres" if has_vres else "inproj",
    )(*args)


def _stack_heads(x, lane_head0):
    return jnp.concatenate([jnp.where(lane_head0, x, 0.0), jnp.where(lane_head0, 0.0, x)], axis=0)


def _mix_kernel(*refs, layer, final, n_chunks, nb, rw_w, hg_w, lru_w):
    it = iter(refs)
    pack_ref, v_ref, bonus_ref, elast_ref, u_ref, x_ref, wout_ref = (next(it) for _ in range(7))
    fw_ref = next(it) if final else None
    lnw_ref, lnb_ref, lbraw_ref, hgnw_ref = (next(it) for _ in range(4))
    convw_ref, convb_ref, wa_ref, ba_ref, wx_ref, bx_ref, lam_ref = (next(it) for _ in range(7))
    hgm_ref, seg256_ref = (next(it) for _ in range(2))
    xout_ref = next(it)
    out_ref, rwstate_ref, hgstate_ref, lrux_ref, lruh_ref = (next(it) for _ in range(5))

    n_pairs = rw_w // LANES
    rows_all = nb * CHUNK
    c2 = 2 * CHUNK
    o_rwz = 0
    o_hgq = o_rwz + rw_w
    o_hgf = o_hgq + hg_w
    o_hgi = o_hgf + hg_w
    o_hgz = o_hgi + hg_w
    o_lrx = o_hgz + hg_w
    o_lrz = o_lrx + lru_w

    @pl.when(pl.program_id(0) == 0)
    def _reset():
        rwstate_ref[...] = jnp.zeros_like(rwstate_ref)
        hgstate_ref[...] = jnp.zeros_like(hgstate_ref)
        lrux_ref[...] = jnp.zeros_like(lrux_ref)
        lruh_ref[...] = jnp.zeros_like(lruh_ref)

    row = _iota((rows_all, 1), 0)
    tpos = row & (CHUNK - 1)
    lane_head0 = _iota((1, LANES), 1) < HEAD
    si = _iota((c2, c2), 0)
    sj = _iota((c2, c2), 1)
    same_head = (si >> LOG2_CHUNK) == (sj >> LOG2_CHUNK)
    ti = si & (CHUNK - 1)
    tj = sj & (CHUNK - 1)
    strict_lower = same_head & (tj < ti)
    incl_lower = same_head & (tj <= ti)
    eye = jnp.where(si == sj, 1.0, 0.0)
    same_group = {size: (si >> (size.bit_length() - 1)) == (sj >> (size.bit_length() - 1)) for size in HG_LEVELS}
    bd_state = (_iota((LANES, LANES), 0) >> LOG2_HEAD) == (_iota((LANES, LANES), 1) >> LOG2_HEAD)
    insts = [(b, p) for p in range(n_pairs) for b in range(nb)]

    seg256 = seg256_ref[...]

    def segsum(x):
        return _head_sums(x, seg256)

    def per_batch(ref):
        val = ref[0]
        for b in range(1, nb):
            val = jnp.where(row >= b * CHUNK, ref[b], val)
        return val

    lbraw = lbraw_ref[...]
    lbe = jnp.exp(lbraw - jnp.max(lbraw, axis=0, keepdims=True))
    lbw = lbe / jnp.sum(lbe, axis=0, keepdims=True)
    lb = lbw[0:1, :]
    for j in range(1, layer + 1):
        lb = lb + lbw[j:j + 1, :]
    lb = lb - lbw[0:1, :]

    def blk(x, b, p):
        return x[b * CHUNK:(b + 1) * CHUNK, p * LANES:(p + 1) * LANES]

    def chunk_body(c, carry):
        r0 = pl.multiple_of(c * CHUNK, CHUNK)
        rows = pl.ds(r0, CHUNK)

        def tokens(ref, lo, hi):
            return jnp.concatenate([ref[b, rows, lo:hi] for b in range(nb)], axis=0)

        def load(lo, hi):
            return tokens(u_ref, lo, hi)

        def store(lo, val):
            for b in range(nb):
                out_ref[b, rows, lo:lo + val.shape[1]] = val[b * CHUNK:(b + 1) * CHUNK].astype(out_ref.dtype)

        def rwkv_stages():
            a_t = tokens(pack_ref, 0, rw_w)
            r_t = tokens(pack_ref, rw_w, 2 * rw_w)
            b_t = tokens(pack_ref, 2 * rw_w, 3 * rw_w)
            k_t = tokens(pack_ref, 3 * rw_w, 4 * rw_w)
            v = tokens(v_ref, 0, rw_w)
            lhs, rhs, vs, a_ak, lrk, avs, pw, tinv = {}, {}, {}, {}, {}, {}, {}, {}
            for i in insts:
                lhs[i] = jnp.concatenate([_stack_heads(blk(a_t, *i), lane_head0),
                                          _stack_heads(blk(r_t, *i), lane_head0)], axis=0)
                rhs[i] = jnp.concatenate([_stack_heads(blk(b_t, *i), lane_head0),
                                          _stack_heads(blk(k_t, *i), lane_head0)], axis=0)
                vs[i] = _stack_heads(blk(v, *i), lane_head0)
            yield
            for i in insts:
                sc = _dot_nt(lhs[i], rhs[i])
                pw[i] = jnp.where(strict_lower, sc[0:c2, 0:c2], 0.0)
                a_ak[i] = jnp.where(strict_lower, sc[0:c2, c2:2 * c2], 0.0).astype(BF16)
                lrk[i] = jnp.concatenate([jnp.where(incl_lower, sc[c2:2 * c2, 0:c2], 0.0),
                                          jnp.where(incl_lower, sc[c2:2 * c2, c2:2 * c2], 0.0)], axis=1).astype(BF16)
            yield
            for i in insts:
                avs[i] = _dot(a_ak[i], vs[i])
                tinv[i] = eye + pw[i]
                pw[i] = _mm(pw[i], pw[i])
            yield
            for s in range(1, 5):
                for i in insts:
                    both = _mm(jnp.concatenate([pw[i], tinv[i]], axis=0), pw[i])
                    pw[i] = both[0:c2]
                    tinv[i] = tinv[i] + both[c2:2 * c2]
                yield
            for i in insts:
                tinv[i] = (tinv[i] + _mm(tinv[i], pw[i])).astype(BF16)
            yield
            st, am, uv, yb = {}, {}, {}, {}
            for n, i in enumerate(insts):
                st[i] = rwstate_ref[n]
                am[i] = _dot_nt(lhs[i], st[i].astype(BF16))
            yield
            for i in insts:
                x = am[i][0:c2] + avs[i]
                uv[i] = jnp.concatenate([_dot(tinv[i], x.astype(BF16)).astype(BF16), vs[i]], axis=0)
            yield
            for i in insts:
                yy = am[i][c2:2 * c2] + _dot(lrk[i], uv[i])
                yb[i] = yy[0:CHUNK] + yy[CHUNK:c2]
            yield
            for n, i in enumerate(insts):
                b, p = i
                rwstate_ref[n] = (st[i] + _dot_tn(uv[i], rhs[i])) * elast_ref[b, c][:, p * LANES:(p + 1) * LANES]
            yield
            y = jnp.concatenate([jnp.concatenate([yb[(b, p)] for p in range(n_pairs)], axis=1) for b in range(nb)], axis=0)
            mean = segsum(y) * (1.0 / HEAD)
            yc = y - mean
            yield
            var = segsum(yc * yc) * (1.0 / HEAD)
            yn = yc * lax.rsqrt(var + RW_GN_EPS) * lnw_ref[...] + lnb_ref[...]
            store(0, (yn + tokens(bonus_ref, 0, rw_w)) * _silu(load(o_rwz, o_rwz + rw_w)))

        def hgrn_stages():
            q = load(o_hgq, o_hgq + hg_w)
            f = lb + (1.0 - lb) * _sigmoid(load(o_hgf, o_hgf + hg_w))
            g = jnp.log2(f)
            kx = 1.0 - f
            iv = load(o_hgi, o_hgi + hg_w)
            bb = _dot(hgm_ref[...], jnp.concatenate(_split2(g), axis=0))
            part = lambda n: bb[n * rows_all:(n + 1) * rows_all]
            bsub = part(0)
            n_lv = len(HG_LEVELS)
            yield
            bchunk = part(n_lv + 1)
            qc = (q * jnp.exp2(bchunk)).astype(BF16)
            kc = (kx * jnp.exp2(part(n_lv + 2))).astype(BF16)
            ivb = iv.astype(BF16)
            ocs = {}
            for n, i in enumerate(insts):
                b, p = i
                hst = hgstate_ref[n]
                ocs[i] = _dot_nt(blk(qc, *i), hst.astype(BF16))
                upd = jnp.where(bd_state, _dot_tn(blk(ivb, *i), blk(kc, *i)), 0.0)
                last = (b + 1) * CHUNK - 1
                hgstate_ref[n] = hst * jnp.exp2(bchunk[last:last + 1, p * LANES:(p + 1) * LANES]) + upd
            yield
            scs = {i: None for i in insts}
            for lv, size in enumerate(HG_LEVELS):
                pst = jnp.exp2(part(1 + lv))
                upper = (row & (size - 1)) >= size // 2
                qs = jnp.where(upper, q * pst, 0.0)
                ks = jnp.where(upper, 0.0, kx * pst)
                for i in insts:
                    sc = jnp.where(same_group[size], _dot_nt(_stack_heads(blk(qs, *i), lane_head0).astype(BF16),
                                                             _stack_heads(blk(ks, *i), lane_head0).astype(BF16)), 0.0)
                    scs[i] = sc if scs[i] is None else scs[i] + sc
                yield
            ostr = {}
            for i in insts:
                ov = _mm(scs[i], _stack_heads(blk(iv, *i), lane_head0))
                ostr[i] = ov[0:CHUNK] + ov[CHUNK:c2] + ocs[i]
            yield
            sub = (rows_all // SUBLANES, SUBLANES, hg_w)
            pos = _iota((1, SUBLANES, 1), 1) & (HG_SUB - 1)
            q3, b3, k3, v3 = (z.reshape(sub) for z in (q, bsub, kx, iv))
            acc = segsum(q * kx) * iv
            for d in range(1, HG_SUB):
                wgt = jnp.where(pos >= d, q3 * jnp.exp2(b3 - pltpu.roll(b3, d, axis=1)) * pltpu.roll(k3, d, axis=1), 0.0)
                acc = acc + segsum(wgt.reshape(rows_all, hg_w)) * pltpu.roll(v3, d, axis=1).reshape(rows_all, hg_w)
                if d % 2 == 1:
                    yield
            o = acc + jnp.concatenate(
                [jnp.concatenate([ostr[(b, p)] for p in range(n_pairs)], axis=1) for b in range(nb)], axis=0)
            ms = segsum(o * o) * (1.0 / HEAD)
            store(rw_w, o * lax.rsqrt(ms + NORM_EPS) * hgnw_ref[...] * _silu(load(o_hgz, o_hgz + hg_w)))

        def lru_stages():
            xb = load(o_lrx, o_lrx + lru_w)
            row8 = _iota((SUBLANES, 1), 0)
            yv = convb_ref[...] + convw_ref[CONV_WIDTH - 1:CONV_WIDTH, :] * xb
            for d in range(1, CONV_WIDTH):
                rolled = pltpu.roll(xb, d, axis=0)
                pieces = []
                for b in range(nb):
                    tail = pltpu.roll(lrux_ref[b], d, axis=0)
                    pieces.append(jnp.where(row8 < d, tail, rolled[b * CHUNK:b * CHUNK + SUBLANES]))
                    pieces.append(rolled[b * CHUNK + SUBLANES:(b + 1) * CHUNK])
                yv = yv + convw_ref[CONV_WIDTH - 1 - d:CONV_WIDTH - d, :] * jnp.concatenate(pieces, axis=0)
            for b in range(nb):
                lrux_ref[b] = xb[(b + 1) * CHUNK - SUBLANES:(b + 1) * CHUNK]
            yield
            ybf = yv.astype(BF16)
            rg = _sigmoid(_dot(ybf, wa_ref[...]) + ba_ref[...])
            ig = _sigmoid(_dot(ybf, wx_ref[...]) + bx_ref[...])
            log_a = -LRU_C * rg * _softplus(-lam_ref[...])
            av = jnp.exp(log_a)
            th = jnp.tanh(log_a)
            gu = jnp.sqrt(-2.0 * th / (1.0 - th)) * (ig * yv)
            yield
            step = 1
            while step < CHUNK:
                keep = tpos >= step
                a_sh = jnp.where(keep, pltpu.roll(av, step, axis=0), 1.0)
                u_sh = jnp.where(keep, pltpu.roll(gu, step, axis=0), 0.0)
                gu = av * u_sh + gu
                av = av * a_sh
                step *= 2
                yield
            hv = gu + av * per_batch(lruh_ref)
            for b in range(nb):
                lruh_ref[b] = hv[(b + 1) * CHUNK - 1:(b + 1) * CHUNK]
            store(rw_w + hg_w, hv * _silu(load(o_lrz, o_lrz + lru_w)))

        pending = [rwkv_stages(), hgrn_stages(), lru_stages()]
        while pending:
            for gen in list(pending):
                if next(gen, True):
                    pending.remove(gen)
        return carry

    lax.fori_loop(0, n_chunks, chunk_body, 0)

    for b in range(nb):
        xn = x_ref[b] + _dot(out_ref[b], wout_ref[...])
        if final:
            ms = jnp.mean(xn * xn, axis=-1, keepdims=True)
            xn = xn * lax.rsqrt(ms + NORM_EPS) * fw_ref[...]
        xout_ref[b] = xn


def _mixer_constants(nb):
    rows_all = nb * CHUNK
    t = np.arange(rows_all)
    same_chunk = (t[None, :] // CHUNK) == (t[:, None] // CHUNK)
    lower = t[None, :] <= t[:, None]
    tri2 = np.tile(lower & same_chunk, (1, 2))
    mats = [lower & ((t[None, :] // HG_SUB) == (t[:, None] // HG_SUB))]
    for size in HG_LEVELS:
        half = size // 2
        same = (t[None, :] // size) == (t[:, None] // size)
        upper_row = (t[:, None] % size) >= half
        upper_col = (t[None, :] % size) >= half
        mats.append(same & np.where(upper_row, upper_col & lower, (~upper_col) & (~lower)))
    mats.append(lower & same_chunk)
    mats.append((~lower) & same_chunk)
    hgm = np.tile(np.concatenate(mats, axis=0), (1, 2))
    h = np.arange(2 * LANES) // HEAD
    seg256 = h[None, :] == h[:, None]
    as_bf16 = lambda m: jnp.asarray(m.astype(np.float32), dtype=BF16)
    return as_bf16(tri2), as_bf16(hgm), as_bf16(seg256)


def _block_diag(w):
    nl, g, n, _ = w.shape
    out = jnp.zeros((nl, g * n, g * n), w.dtype)
    for i in range(g):
        out = out.at[:, i * n:(i + 1) * n, i * n:(i + 1) * n].set(w[:, i])
    return out


def _mixer(prep, x2, w_out_bf16, final_w, layer, p, batch, seq):
    pack, vb, bonus, elast, rest = prep
    d = x2.shape[1]
    rw_w = vb.shape[1]
    hg_w = p["hgnw"][0].shape[-1]
    lru_w = p["lam"][0].shape[-1]
    d_mix = rw_w + hg_w + lru_w
    tb = _row_block(seq, 256)
    assert tb % CHUNK == 0 and rw_w == 3 * LANES and hg_w == rw_w and w_out_bf16[0].shape[-2] == d_mix

    def tok_spec(width):
        return pl.BlockSpec((batch, tb, width), lambda t: (0, t, 0))

    def full_spec(a):
        nd = a.ndim
        return pl.BlockSpec(a.shape, lambda t: (0,) * nd)

    per_token = lambda a: a.reshape(batch, seq, a.shape[1])
    args = [per_token(pack), per_token(vb), per_token(bonus), elast.reshape(batch, seq // CHUNK, 1, rw_w),
            per_token(rest), per_token(x2), w_out_bf16[0]]
    in_specs = [tok_spec(4 * rw_w), tok_spec(rw_w), tok_spec(rw_w),
                pl.BlockSpec((batch, tb // CHUNK, 1, rw_w), lambda t: (0, t, 0, 0)), tok_spec(rest.shape[1]),
                tok_spec(d), _layer_spec(*w_out_bf16)]
    if final_w is not None:
        args.append(final_w)
        in_specs.append(full_spec(final_w))
    names = ["lnw", "lnb", "lbraw", "hgnw", "convw", "convb", "wa", "ba", "wx", "bx", "lam"]
    for k in names:
        a, idx = p[k]
        args.append(a)
        in_specs.append(full_spec(a) if idx is None else _layer_spec(a, idx))
    _, hgm, seg256 = _mixer_constants(batch)
    for a in (hgm, seg256):
        args.append(a)
        in_specs.append(full_spec(a))

    n_pairs = rw_w // LANES
    scratch = [pltpu.VMEM((batch, tb, d_mix), BF16),
               pltpu.VMEM((batch * n_pairs, LANES, LANES), F32),
               pltpu.VMEM((batch * n_pairs, LANES, LANES), F32),
               pltpu.VMEM((batch, SUBLANES, lru_w), F32),
               pltpu.VMEM((batch, 1, lru_w), F32)]
    block_bytes = (batch * tb * (rest.shape[1] * 4 + 4 * rw_w * 2 + rw_w * 2 + rw_w * 4 + d_mix * 2 + 2 * d * 4)
                   + d_mix * d * 2)
    body = functools.partial(_mix_kernel, layer=layer, final=final_w is not None, n_chunks=tb // CHUNK, nb=batch,
                             rw_w=rw_w, hg_w=hg_w, lru_w=lru_w)
    x_new = pl.pallas_call(
        body,
        out_shape=jax.ShapeDtypeStruct((batch, seq, d), F32),
        grid=(seq // tb,),
        in_specs=in_specs,
        out_specs=tok_spec(d),
        scratch_shapes=scratch,
        compiler_params=pltpu.CompilerParams(
            dimension_semantics=("arbitrary",), vmem_limit_bytes=_vmem_limit(block_bytes)),
        name=f"mixer_l{layer}",
    )(*args)
    return x_new.reshape(batch * seq, d)


def kernel(x, norm_w, w_in, rw_mu, rw_w0, rw_w_up, rw_a0, rw_a_up, rw_v0, rw_v_dn, rw_v_up, rw_k_k, rw_k_a, rw_r_k, rw_ln_w, rw_ln_b, hg_lb_raw, hg_norm_w, lru_conv_w, lru_conv_b, lru_wa, lru_ba, lru_wx, lru_bx, lru_lambda, w_out, final_norm_w):
    batch, seq, d = x.shape
    depth = w_in.shape[0]
    rw_w = rw_w0.shape[1]
    lora = rw_w_up.shape[1]
    assert 2 * lora == LANES
    x2 = x.reshape(batch * seq, d)
    zeros_code = jnp.zeros((depth, lora, rw_w), F32)
    per_layer = lambda a: a.reshape(a.shape[0], 1, -1)
    stacked = {
        "norm": per_layer(norm_w), "win": w_in.astype(BF16), "wout": w_out.astype(BF16),
        "mu": per_layer(rw_mu), "w0": per_layer(rw_w0), "a0": per_layer(rw_a0),
        "wcode": jnp.concatenate([jnp.concatenate([rw_w_up, zeros_code], axis=2),
                                  jnp.concatenate([zeros_code, rw_a_up], axis=2)], axis=1).astype(BF16),
        "kk": per_layer(rw_k_k), "ka": per_layer(rw_k_a), "rk": per_layer(rw_r_k),
        "lnw": per_layer(rw_ln_w), "lnb": per_layer(rw_ln_b), "hgnw": per_layer(hg_norm_w),
        "convw": lru_conv_w, "convb": per_layer(lru_conv_b),
        "wa": _block_diag(lru_wa).astype(BF16), "ba": per_layer(lru_ba),
        "wx": _block_diag(lru_wx).astype(BF16), "bx": per_layer(lru_bx), "lam": per_layer(lru_lambda),
    }
    vres = {"v0": per_layer(rw_v0), "vdn": rw_v_dn.astype(BF16), "vup": rw_v_up.astype(BF16)}
    v_first = None
    for l in range(depth):
        p = {k: (a, l) for k, a in stacked.items()}
        p["lbraw"] = (hg_lb_raw, None)
        if l > 0:
            p.update({k: (a, l - 1) for k, a in vres.items()})
        prep = _inproj(x2, p["norm"], p["win"], v_first, p, seq)
        if l == 0:
            v_first = prep[5]
        fw = final_norm_w.reshape(1, -1) if l == depth - 1 else None
        x2 = _mixer(prep[:5], x2, p["wout"], fw, l, p, batch, seq)
    return x2.reshape(batch, seq, d)
```

```python
import functools

import numpy as np
import jax
import jax.numpy as jnp
from jax import lax
from jax.experimental import pallas as pl
from jax.experimental.pallas import tpu as pltpu

F32 = jnp.float32
BF16 = jnp.bfloat16

NORM_EPS = 1e-6
RW_GN_EPS = 64e-5
LRU_C = 8.0
HEAD = 64
LANES = 128
SUBLANES = 8
CHUNK = 64
HG_SUB = 4
HG_LEVELS = (8, 16, 32, 64)
CONV_WIDTH = 4
LOG2_HEAD = HEAD.bit_length() - 1
LOG2_CHUNK = CHUNK.bit_length() - 1
LOG2_E = 1.4426950408889634
CHUNKS_PER_ITER = 2
V7X_VMEM_BYTES = 64 * 1024 * 1024


def _dot(a, b):
    return jnp.dot(a, b, preferred_element_type=F32)


def _dot_nt(a, b):
    return lax.dot_general(a, b, (((1,), (1,)), ((), ())), preferred_element_type=F32)


def _dot_tn(a, b):
    return lax.dot_general(a, b, (((0,), (0,)), ((), ())), preferred_element_type=F32)


def _mm(a, b):
    return _dot(a.astype(BF16), b.astype(BF16))


def _split2(x):
    hi = x.astype(BF16)
    lo = (x - hi.astype(F32)).astype(BF16)
    return hi, lo


def _sigmoid(x):
    return 1.0 / (1.0 + jnp.exp(-x))


def _silu(x):
    return x * _sigmoid(x)


def _softplus(x):
    return jnp.maximum(x, 0.0) + jnp.log1p(jnp.exp(-jnp.abs(x)))


def _iota(shape, dim):
    return lax.broadcasted_iota(jnp.int32, shape, dim)


def _head_sums(x, seg256):
    xb = x.astype(BF16)
    seg128 = seg256[0:LANES, 0:LANES]
    return jnp.concatenate([_dot(xb[:, 0:2 * LANES], seg256), _dot(xb[:, 2 * LANES:3 * LANES], seg128)], axis=1)


def _inproj_kernel(*refs, has_vres, rw_w, tiles_per_seq):
    it = iter(refs)
    x_ref, nw_ref, w_ref = (next(it) for _ in range(3))
    vfirst_in_ref = next(it) if has_vres else None
    mu_ref, w0_ref, a0_ref, wcode_ref = (next(it) for _ in range(4))
    if has_vres:
        v0_ref, vdn_ref, vup_ref = (next(it) for _ in range(3))
    kk_ref, ka_ref, rk_ref, tri2_ref, seg256_ref = (next(it) for _ in range(5))
    pack_ref, v_ref, bonus_ref, elast_ref, rest_ref = (next(it) for _ in range(5))
    vfirst_out_ref = None if has_vres else next(it)
    prev_ref = next(it)

    tm = x_ref.shape[0]
    sub = 2 * CHUNK
    n_sub = tm // sub
    rw_shift = 3 * rw_w + LANES
    rest_tiles = (w_ref.shape[1] - rw_shift) // LANES

    @pl.when(lax.rem(pl.program_id(0), tiles_per_seq) == 0)
    def _new_sequence():
        prev_ref[...] = jnp.zeros_like(prev_ref)

    x = x_ref[...]
    ms = jnp.mean(x * x, axis=-1, keepdims=True)
    hb = (x * lax.rsqrt(ms + NORM_EPS) * nw_ref[...]).astype(BF16)
    u_rw = _dot(hb, w_ref[:, 0:rw_shift])

    seg256 = seg256_ref[...]
    row = _iota((sub, 1), 0)
    lane_first_half = _iota((1, LANES), 1) < LANES // 2
    prev = prev_ref[...]
    for s in range(n_sub):
        lo = rw_shift + (rest_tiles * s // n_sub) * LANES
        hi = rw_shift + (rest_tiles * (s + 1) // n_sub) * LANES
        rest_ref[:, lo - rw_shift:hi - rw_shift] = _dot(hb, w_ref[:, lo:hi])

        rs = slice(s * sub, (s + 1) * sub)
        ru = u_rw[rs]
        sh = jnp.where(row == 0, prev, pltpu.roll(ru, 1, axis=0))
        prev = ru[sub - 1:sub, :]
        ul = ru + mu_ref[...] * (sh - ru)
        r = ul[:, 0:rw_w]
        k = ul[:, rw_w:2 * rw_w]
        v = ul[:, 2 * rw_w:3 * rw_w]
        codes = ul[:, 3 * rw_w:rw_shift]
        both = _mm(jnp.where(lane_first_half, jnp.tanh(codes), codes), wcode_ref[...])
        w_log = -_softplus(-(w0_ref[...] + both[:, 0:rw_w])) - 0.5
        nlw2 = jnp.exp(w_log) * LOG2_E
        a = _sigmoid(a0_ref[...] + both[:, rw_w:2 * rw_w])
        if has_vres:
            gate = _sigmoid(v0_ref[...] + _mm(_mm(v, vdn_ref[...]), vup_ref[...]))
            v = v + (vfirst_in_ref[rs, :] - v) * gate
        else:
            vfirst_out_ref[rs, :] = v
        kk = k * kk_ref[...]
        kk = kk / jnp.maximum(jnp.sqrt(_head_sums(kk * kk, seg256)), 1e-12)
        k2 = k * (1.0 + (a - 1.0) * ka_ref[...])
        bonus_ref[rs, :] = _head_sums(r * k2 * rk_ref[...], seg256) * v
        v_ref[rs, :] = v.astype(BF16)
        cs = _dot(tri2_ref[...], jnp.concatenate(_split2(nlw2), axis=0))
        e_pos = jnp.exp2(-cs)
        e_neg = jnp.exp2(cs)
        pack_ref[rs, 0:rw_w] = (-kk * jnp.exp2(nlw2 - cs)).astype(BF16)
        pack_ref[rs, rw_w:2 * rw_w] = (r * e_pos).astype(BF16)
        pack_ref[rs, 2 * rw_w:3 * rw_w] = (kk * a * e_neg).astype(BF16)
        pack_ref[rs, 3 * rw_w:4 * rw_w] = (k2 * e_neg).astype(BF16)
        for c in range(sub // CHUNK):
            n = s * (sub // CHUNK) + c
            elast_ref[n:n + 1, :] = e_pos[(c + 1) * CHUNK - 1:(c + 1) * CHUNK, :]
    prev_ref[...] = prev


def _row_block(n_rows, target):
    blk = min(target, n_rows)
    while n_rows % blk:
        blk //= 2
    return blk


def _vmem_limit(block_bytes):
    return int(min(V7X_VMEM_BYTES * 7 // 8, 2 * block_bytes + 16 * 1024 * 1024))


def _layer_spec(a, idx):
    nd = a.ndim
    return pl.BlockSpec((None,) + a.shape[1:], lambda *_: (idx,) + (0,) * (nd - 1))


def _inproj(x2, norm_w, w_bf16, v_first, p, seq):
    n, d = x2.shape
    n_in = w_bf16[0].shape[-1]
    rw_w = p["w0"][0].shape[-1]
    rw_shift = 3 * rw_w + LANES
    tm = _row_block(seq, 512)
    assert tm % (SUBLANES * CHUNK) == 0 and seq % tm == 0
    has_vres = v_first is not None

    def tok_spec(width):
        return pl.BlockSpec((tm, width), lambda i: (i, 0))

    def full_spec(a):
        nd = a.ndim
        return pl.BlockSpec(a.shape, lambda i: (0,) * nd)

    tri2, _, seg256 = _mixer_constants(2)
    args, in_specs = [x2], [tok_spec(d)]
    for a, idx in (norm_w, w_bf16):
        args.append(a)
        in_specs.append(_layer_spec(a, idx))
    if has_vres:
        args.append(v_first)
        in_specs.append(tok_spec(rw_w))
    names = ["mu", "w0", "a0", "wcode"] + (["v0", "vdn", "vup"] if has_vres else []) + ["kk", "ka", "rk"]
    for a, idx in [p[k] for k in names]:
        args.append(a)
        in_specs.append(_layer_spec(a, idx))
    for a in (tri2, seg256):
        args.append(a)
        in_specs.append(full_spec(a))
    out_shape = [jax.ShapeDtypeStruct((n, 4 * rw_w), BF16), jax.ShapeDtypeStruct((n, rw_w), BF16),
                 jax.ShapeDtypeStruct((n, rw_w), F32), jax.ShapeDtypeStruct((n // CHUNK, rw_w), F32),
                 jax.ShapeDtypeStruct((n, n_in - rw_shift), F32)]
    out_specs = [tok_spec(4 * rw_w), tok_spec(rw_w), tok_spec(rw_w),
                 pl.BlockSpec((tm // CHUNK, rw_w), lambda i: (i, 0)), tok_spec(n_in - rw_shift)]
    if not has_vres:
        out_shape.append(jax.ShapeDtypeStruct((n, rw_w), F32))
        out_specs.append(tok_spec(rw_w))
    block_bytes = tm * d * 4 + d * n_in * 2 + tm * (n_in + 4 * rw_w) * 4
    body = functools.partial(_inproj_kernel, has_vres=has_vres, rw_w=rw_w, tiles_per_seq=seq // tm)
    return pl.pallas_call(
        body,
        out_shape=out_shape,
        grid=(n // tm,),
        in_specs=in_specs,
        out_specs=out_specs,
        scratch_shapes=[pltpu.VMEM((1, rw_shift), F32)],
        compiler_params=pltpu.CompilerParams(
            dimension_semantics=("arbitrary",), vmem_limit_bytes=_vmem_limit(block_bytes)),
        name="inproj_vres" if has_vres else "inproj",
    )(*args)


def _stack_heads(x, lane_head0):
    return jnp.concatenate([jnp.where(lane_head0, x, 0.0), jnp.where(lane_head0, 0.0, x)], axis=0)


def _mix_kernel(*refs, layer, final, n_chunks, nb, rw_w, hg_w, lru_w):
    it = iter(refs)
    pack_ref, v_ref, bonus_ref, elast_ref, u_ref, x_ref, wout_ref = (next(it) for _ in range(7))
    fw_ref = next(it) if final else None
    lnw_ref, lnb_ref, lbraw_ref, hgnw_ref = (next(it) for _ in range(4))
    convw_ref, convb_ref, wa_ref, ba_ref, wx_ref, bx_ref, lam_ref = (next(it) for _ in range(7))
    hgm_ref, seg256_ref = (next(it) for _ in range(2))
    xout_ref = next(it)
    out_ref, rwstate_ref, hgstate_ref, lrux_ref, lruh_ref = (next(it) for _ in range(5))

    n_pairs = rw_w // LANES
    rows_all = nb * CHUNK
    o_rwz = 0
    o_hgq = o_rwz + rw_w
    o_hgf = o_hgq + hg_w
    o_hgi = o_hgf + hg_w
    o_hgz = o_hgi + hg_w
    o_lrx = o_hgz + hg_w
    o_lrz = o_lrx + lru_w

    @pl.when(pl.program_id(0) == 0)
    def _reset():
        rwstate_ref[...] = jnp.zeros_like(rwstate_ref)
        hgstate_ref[...] = jnp.zeros_like(hgstate_ref)
        lrux_ref[...] = jnp.zeros_like(lrux_ref)
        lruh_ref[...] = jnp.zeros_like(lruh_ref)

    row = _iota((rows_all, 1), 0)
    tpos = row & (CHUNK - 1)
    lane_head0 = _iota((1, LANES), 1) < HEAD
    ti = _iota((CHUNK, LANES), 0)
    tj = _iota((CHUNK, LANES), 1) & (CHUNK - 1)
    strict_lower = tj < ti
    incl_lower = tj <= ti
    eye = jnp.where(tj == ti, 1.0, 0.0)
    same_group = {size: (ti >> (size.bit_length() - 1)) == (tj >> (size.bit_length() - 1)) for size in HG_LEVELS}
    bd_state = (_iota((LANES, LANES), 0) >> LOG2_HEAD) == (_iota((LANES, LANES), 1) >> LOG2_HEAD)
    pairs = [(b, p) for p in range(n_pairs) for b in range(nb)]

    seg256 = seg256_ref[...]

    def segsum(x):
        return _head_sums(x, seg256)

    def stacked(m):
        return _stack_heads(m, lane_head0).astype(BF16)

    def per_batch(ref):
        val = ref[0]
        for b in range(1, nb):
            val = jnp.where(row >= b * CHUNK, ref[b], val)
        return val

    def tokens(ref, rows, lo, hi):
        return jnp.concatenate([ref[b, rows, lo:hi] for b in range(nb)], axis=0)

    def store(rows, lo, val):
        for b in range(nb):
            out_ref[b, rows, lo:lo + val.shape[1]] = val[b * CHUNK:(b + 1) * CHUNK].astype(out_ref.dtype)

    def blk(x, b, p):
        return x[b * CHUNK:(b + 1) * CHUNK, p * LANES:(p + 1) * LANES]

    lbraw = lbraw_ref[...]
    lbe = jnp.exp(lbraw - jnp.max(lbraw, axis=0, keepdims=True))
    lbw = lbe / jnp.sum(lbe, axis=0, keepdims=True)
    lb = lbw[0:1, :]
    for j in range(1, layer + 1):
        lb = lb + lbw[j:j + 1, :]
    lb = lb - lbw[0:1, :]

    def rwkv_stages(chunks):
        insts = [(ci, b, p) for ci in range(len(chunks)) for (b, p) in pairs]
        lhs, rhs, vs, a_ak, lrk, avs, pw, tinv = {}, {}, {}, {}, {}, {}, {}, {}
        for ci, (_, rows) in enumerate(chunks):
            a_t = tokens(pack_ref, rows, 0, rw_w)
            r_t = tokens(pack_ref, rows, rw_w, 2 * rw_w)
            b_t = tokens(pack_ref, rows, 2 * rw_w, 3 * rw_w)
            k_t = tokens(pack_ref, rows, 3 * rw_w, 4 * rw_w)
            v = tokens(v_ref, rows, 0, rw_w)
            for b, p in pairs:
                i = (ci, b, p)
                lhs[i] = jnp.concatenate([blk(a_t, b, p), blk(r_t, b, p)], axis=0)
                rhs[i] = jnp.concatenate([_stack_heads(blk(b_t, b, p), lane_head0),
                                          _stack_heads(blk(k_t, b, p), lane_head0)], axis=0)
                vs[i] = _stack_heads(blk(v, b, p), lane_head0)
        yield
        for i in insts:
            sc = _dot_nt(lhs[i], rhs[i])
            pw[i] = jnp.where(strict_lower, sc[0:CHUNK, 0:LANES], 0.0)
            a_ak[i] = jnp.where(strict_lower, sc[0:CHUNK, LANES:2 * LANES], 0.0).astype(BF16)
            lrk[i] = jnp.concatenate([jnp.where(incl_lower, sc[CHUNK:2 * CHUNK, 0:LANES], 0.0),
                                      jnp.where(incl_lower, sc[CHUNK:2 * CHUNK, LANES:2 * LANES], 0.0)],
                                     axis=1).astype(BF16)
        yield
        for i in insts:
            avs[i] = _dot(a_ak[i], vs[i])
            tinv[i] = eye + pw[i]
            pw[i] = _dot(pw[i].astype(BF16), stacked(pw[i]))
        yield
        for s in range(1, 5):
            for i in insts:
                both = _dot(jnp.concatenate([pw[i], tinv[i]], axis=0).astype(BF16), stacked(pw[i]))
                pw[i] = both[0:CHUNK]
                tinv[i] = tinv[i] + both[CHUNK:2 * CHUNK]
            yield
        for i in insts:
            tinv[i] = (tinv[i] + _dot(tinv[i].astype(BF16), stacked(pw[i]))).astype(BF16)
        yield
        for ci, (c, rows) in enumerate(chunks):
            st, am, uv, yb = {}, {}, {}, {}
            for n, (b, p) in enumerate(pairs):
                st[b, p] = rwstate_ref[n]
                am[b, p] = _dot_nt(lhs[ci, b, p], st[b, p].astype(BF16))
            yield
            for b, p in pairs:
                i = (ci, b, p)
                x = am[b, p][0:CHUNK] + avs[i]
                uv[b, p] = jnp.concatenate([stacked(_dot(tinv[i], stacked(x))), vs[i]], axis=0)
            yield
            for b, p in pairs:
                yb[b, p] = am[b, p][CHUNK:2 * CHUNK] + _dot(lrk[ci, b, p], uv[b, p])
            yield
            for n, (b, p) in enumerate(pairs):
                rwstate_ref[n] = ((st[b, p] + _dot_tn(uv[b, p], rhs[ci, b, p]))
                                  * elast_ref[b, c][:, p * LANES:(p + 1) * LANES])
            yield
            y = jnp.concatenate([jnp.concatenate([yb[b, p] for p in range(n_pairs)], axis=1) for b in range(nb)], axis=0)
            mean = segsum(y) * (1.0 / HEAD)
            yc = y - mean
            yield
            var = segsum(yc * yc) * (1.0 / HEAD)
            yn = yc * lax.rsqrt(var + RW_GN_EPS) * lnw_ref[...] + lnb_ref[...]
            store(rows, 0, (yn + tokens(bonus_ref, rows, 0, rw_w)) * _silu(tokens(u_ref, rows, o_rwz, o_rwz + rw_w)))
            yield

    def hgrn_stages(c, rows):
        q = tokens(u_ref, rows, o_hgq, o_hgq + hg_w)
        f = lb + (1.0 - lb) * _sigmoid(tokens(u_ref, rows, o_hgf, o_hgf + hg_w))
        g = jnp.log2(f)
        kx = 1.0 - f
        iv = tokens(u_ref, rows, o_hgi, o_hgi + hg_w)
        bb = _dot(hgm_ref[...], jnp.concatenate(_split2(g), axis=0))
        part = lambda n: bb[n * rows_all:(n + 1) * rows_all]
        bsub = part(0)
        n_lv = len(HG_LEVELS)
        yield
        bchunk = part(n_lv + 1)
        qc = (q * jnp.exp2(bchunk)).astype(BF16)
        kc = (kx * jnp.exp2(part(n_lv + 2))).astype(BF16)
        ivb = iv.astype(BF16)
        ocs = {}
        for n, (b, p) in enumerate(pairs):
            hst = hgstate_ref[n]
            ocs[b, p] = _dot_nt(blk(qc, b, p), hst.astype(BF16))
            upd = jnp.where(bd_state, _dot_tn(blk(ivb, b, p), blk(kc, b, p)), 0.0)
            last = (b + 1) * CHUNK - 1
            hgstate_ref[n] = hst * jnp.exp2(bchunk[last:last + 1, p * LANES:(p + 1) * LANES]) + upd
        yield
        scs = {i: None for i in pairs}
        for lv, size in enumerate(HG_LEVELS):
            pst = jnp.exp2(part(1 + lv))
            upper = (row & (size - 1)) >= size // 2
            qs = jnp.where(upper, q * pst, 0.0)
            ks = jnp.where(upper, 0.0, kx * pst)
            for i in pairs:
                sc = jnp.where(same_group[size], _dot_nt(blk(qs, *i).astype(BF16), stacked(blk(ks, *i))), 0.0)
                scs[i] = sc if scs[i] is None else scs[i] + sc
            yield
        ostr = {i: _mm(scs[i], stacked(blk(iv, *i))) + ocs[i] for i in pairs}
        yield
        sub = (rows_all // SUBLANES, SUBLANES, hg_w)
        pos = _iota((1, SUBLANES, 1), 1) & (HG_SUB - 1)
        q3, b3, k3, v3 = (z.reshape(sub) for z in (q, bsub, kx, iv))
        acc = segsum(q * kx) * iv
        for d in range(1, HG_SUB):
            wgt = jnp.where(pos >= d, q3 * jnp.exp2(b3 - pltpu.roll(b3, d, axis=1)) * pltpu.roll(k3, d, axis=1), 0.0)
            acc = acc + segsum(wgt.reshape(rows_all, hg_w)) * pltpu.roll(v3, d, axis=1).reshape(rows_all, hg_w)
            if d % 2 == 1:
                yield
        o = acc + jnp.concatenate(
            [jnp.concatenate([ostr[(b, p)] for p in range(n_pairs)], axis=1) for b in range(nb)], axis=0)
        ms = segsum(o * o) * (1.0 / HEAD)
        store(rows, rw_w, o * lax.rsqrt(ms + NORM_EPS) * hgnw_ref[...] * _silu(tokens(u_ref, rows, o_hgz, o_hgz + hg_w)))

    def lru_stages(c, rows):
        xb = tokens(u_ref, rows, o_lrx, o_lrx + lru_w)
        row8 = _iota((SUBLANES, 1), 0)
        yv = convb_ref[...] + convw_ref[CONV_WIDTH - 1:CONV_WIDTH, :] * xb
        for d in range(1, CONV_WIDTH):
            rolled = pltpu.roll(xb, d, axis=0)
            pieces = []
            for b in range(nb):
                tail = pltpu.roll(lrux_ref[b], d, axis=0)
                pieces.append(jnp.where(row8 < d, tail, rolled[b * CHUNK:b * CHUNK + SUBLANES]))
                pieces.append(rolled[b * CHUNK + SUBLANES:(b + 1) * CHUNK])
            yv = yv + convw_ref[CONV_WIDTH - 1 - d:CONV_WIDTH - d, :] * jnp.concatenate(pieces, axis=0)
        for b in range(nb):
            lrux_ref[b] = xb[(b + 1) * CHUNK - SUBLANES:(b + 1) * CHUNK]
        yield
        ybf = yv.astype(BF16)
        rg = _sigmoid(_dot(ybf, wa_ref[...]) + ba_ref[...])
        ig = _sigmoid(_dot(ybf, wx_ref[...]) + bx_ref[...])
        log_a = -LRU_C * rg * _softplus(-lam_ref[...])
        av = jnp.exp(log_a)
        th = jnp.tanh(log_a)
        gu = jnp.sqrt(-2.0 * th / (1.0 - th)) * (ig * yv)
        yield
        step = 1
        while step < CHUNK:
            keep = tpos >= step
            a_sh = jnp.where(keep, pltpu.roll(av, step, axis=0), 1.0)
            u_sh = jnp.where(keep, pltpu.roll(gu, step, axis=0), 0.0)
            gu = av * u_sh + gu
            av = av * a_sh
            step *= 2
            yield
        hv = gu + av * per_batch(lruh_ref)
        for b in range(nb):
            lruh_ref[b] = hv[(b + 1) * CHUNK - 1:(b + 1) * CHUNK]
        store(rows, rw_w + hg_w, hv * _silu(tokens(u_ref, rows, o_lrz, o_lrz + lru_w)))

    def group_body(gi, carry):
        chunks = []
        for j in range(CHUNKS_PER_ITER):
            c = gi * CHUNKS_PER_ITER + j
            chunks.append((c, pl.ds(pl.multiple_of(c * CHUNK, CHUNK), CHUNK)))
        pending = [rwkv_stages(chunks)] + [hgrn_stages(*ch) for ch in chunks] + [lru_stages(*ch) for ch in chunks]
        while pending:
            for gen in list(pending):
                if next(gen, True):
                    pending.remove(gen)
        return carry

    assert n_chunks % CHUNKS_PER_ITER == 0
    lax.fori_loop(0, n_chunks // CHUNKS_PER_ITER, group_body, 0)

    for b in range(nb):
        xn = x_ref[b] + _dot(out_ref[b], wout_ref[...])
        if final:
            ms = jnp.mean(xn * xn, axis=-1, keepdims=True)
            xn = xn * lax.rsqrt(ms + NORM_EPS) * fw_ref[...]
        xout_ref[b] = xn


def _mixer_constants(nb):
    rows_all = nb * CHUNK
    t = np.arange(rows_all)
    same_chunk = (t[None, :] // CHUNK) == (t[:, None] // CHUNK)
    lower = t[None, :] <= t[:, None]
    tri2 = np.tile(lower & same_chunk, (1, 2))
    mats = [lower & ((t[None, :] // HG_SUB) == (t[:, None] // HG_SUB))]
    for size in HG_LEVELS:
        half = size // 2
        same = (t[None, :] // size) == (t[:, None] // size)
        upper_row = (t[:, None] % size) >= half
        upper_col = (t[None, :] % size) >= half
        mats.append(same & np.where(upper_row, upper_col & lower, (~upper_col) & (~lower)))
    mats.append(lower & same_chunk)
    mats.append((~lower) & same_chunk)
    hgm = np.tile(np.concatenate(mats, axis=0), (1, 2))
    h = np.arange(2 * LANES) // HEAD
    seg256 = h[None, :] == h[:, None]
    as_bf16 = lambda m: jnp.asarray(m.astype(np.float32), dtype=BF16)
    return as_bf16(tri2), as_bf16(hgm), as_bf16(seg256)


def _block_diag(w):
    nl, g, n, _ = w.shape
    out = jnp.zeros((nl, g * n, g * n), w.dtype)
    for i in range(g):
        out = out.at[:, i * n:(i + 1) * n, i * n:(i + 1) * n].set(w[:, i])
    return out


def _mixer(prep, x2, w_out_bf16, final_w, layer, p, batch, seq):
    pack, vb, bonus, elast, rest = prep
    d = x2.shape[1]
    rw_w = vb.shape[1]
    hg_w = p["hgnw"][0].shape[-1]
    lru_w = p["lam"][0].shape[-1]
    d_mix = rw_w + hg_w + lru_w
    tb = _row_block(seq, 256)
    assert tb % (CHUNKS_PER_ITER * CHUNK) == 0 and rw_w == 3 * LANES and hg_w == rw_w
    assert w_out_bf16[0].shape[-2] == d_mix

    def tok_spec(width):
        return pl.BlockSpec((batch, tb, width), lambda t: (0, t, 0))

    def full_spec(a):
        nd = a.ndim
        return pl.BlockSpec(a.shape, lambda t: (0,) * nd)

    per_token = lambda a: a.reshape(batch, seq, a.shape[1])
    args = [per_token(pack), per_token(vb), per_token(bonus), elast.reshape(batch, seq // CHUNK, 1, rw_w),
            per_token(rest), per_token(x2), w_out_bf16[0]]
    in_specs = [tok_spec(4 * rw_w), tok_spec(rw_w), tok_spec(rw_w),
                pl.BlockSpec((batch, tb // CHUNK, 1, rw_w), lambda t: (0, t, 0, 0)), tok_spec(rest.shape[1]),
                tok_spec(d), _layer_spec(*w_out_bf16)]
    if final_w is not None:
        args.append(final_w)
        in_specs.append(full_spec(final_w))
    names = ["lnw", "lnb", "lbraw", "hgnw", "convw", "convb", "wa", "ba", "wx", "bx", "lam"]
    for k in names:
        a, idx = p[k]
        args.append(a)
        in_specs.append(full_spec(a) if idx is None else _layer_spec(a, idx))
    _, hgm, seg256 = _mixer_constants(batch)
    for a in (hgm, seg256):
        args.append(a)
        in_specs.append(full_spec(a))

    n_pairs = rw_w // LANES
    scratch = [pltpu.VMEM((batch, tb, d_mix), BF16),
               pltpu.VMEM((batch * n_pairs, LANES, LANES), F32),
               pltpu.VMEM((batch * n_pairs, LANES, LANES), F32),
               pltpu.VMEM((batch, SUBLANES, lru_w), F32),
               pltpu.VMEM((batch, 1, lru_w), F32)]
    block_bytes = (batch * tb * (rest.shape[1] * 4 + 4 * rw_w * 2 + rw_w * 2 + rw_w * 4 + d_mix * 2 + 2 * d * 4)
                   + d_mix * d * 2)
    body = functools.partial(_mix_kernel, layer=layer, final=final_w is not None, n_chunks=tb // CHUNK, nb=batch,
                             rw_w=rw_w, hg_w=hg_w, lru_w=lru_w)
    x_new = pl.pallas_call(
        body,
        out_shape=jax.ShapeDtypeStruct((batch, seq, d), F32),
        grid=(seq // tb,),
        in_specs=in_specs,
        out_specs=tok_spec(d),
        scratch_shapes=scratch,
        compiler_params=pltpu.CompilerParams(
            dimension_semantics=("arbitrary",), vmem_limit_bytes=_vmem_limit(block_bytes)),
        name=f"mixer_l{layer}",
    )(*args)
    return x_new.reshape(batch * seq, d)


def kernel(x, norm_w, w_in, rw_mu, rw_w0, rw_w_up, rw_a0, rw_a_up, rw_v0, rw_v_dn, rw_v_up, rw_k_k, rw_k_a, rw_r_k, rw_ln_w, rw_ln_b, hg_lb_raw, hg_norm_w, lru_conv_w, lru_conv_b, lru_wa, lru_ba, lru_wx, lru_bx, lru_lambda, w_out, final_norm_w):
    batch, seq, d = x.shape
    depth = w_in.shape[0]
    rw_w = rw_w0.shape[1]
    lora = rw_w_up.shape[1]
    assert 2 * lora == LANES
    x2 = x.reshape(batch * seq, d)
    zeros_code = jnp.zeros((depth, lora, rw_w), F32)
    per_layer = lambda a: a.reshape(a.shape[0], 1, -1)
    stacked = {
        "norm": per_layer(norm_w), "win": w_in.astype(BF16), "wout": w_out.astype(BF16),
        "mu": per_layer(rw_mu), "w0": per_layer(rw_w0), "a0": per_layer(rw_a0),
        "wcode": jnp.concatenate([jnp.concatenate([rw_w_up, zeros_code], axis=2),
                                  jnp.concatenate([zeros_code, rw_a_up], axis=2)], axis=1).astype(BF16),
        "kk": per_layer(rw_k_k), "ka": per_layer(rw_k_a), "rk": per_layer(rw_r_k),
        "lnw": per_layer(rw_ln_w), "lnb": per_layer(rw_ln_b), "hgnw": per_layer(hg_norm_w),
        "convw": lru_conv_w, "convb": per_layer(lru_conv_b),
        "wa": _block_diag(lru_wa).astype(BF16), "ba": per_layer(lru_ba),
        "wx": _block_diag(lru_wx).astype(BF16), "bx": per_layer(lru_bx), "lam": per_layer(lru_lambda),
    }
    vres = {"v0": per_layer(rw_v0), "vdn": rw_v_dn.astype(BF16), "vup": rw_v_up.astype(BF16)}
    v_first = None
    for l in range(depth):
        p = {k: (a, l) for k, a in stacked.items()}
        p["lbraw"] = (hg_lb_raw, None)
        if l > 0:
            p.update({k: (a, l - 1) for k, a in vres.items()})
        prep = _inproj(x2, p["norm"], p["win"], v_first, p, seq)
        if l == 0:
            v_first = prep[5]
        fw = final_norm_w.reshape(1, -1) if l == depth - 1 else None
        x2 = _mixer(prep[:5], x2, p["wout"], fw, l, p, batch, seq)
    return x2.reshape(batch, seq, d)
```

```python
import functools

import numpy as np
import jax
import jax.numpy as jnp
from jax import lax
from jax.experimental import pallas as pl
from jax.experimental.pallas import tpu as pltpu

F32 = jnp.float32
BF16 = jnp.bfloat16

NORM_EPS = 1e-6
RW_GN_EPS = 64e-5
LRU_C = 8.0
HEAD = 64
LANES = 128
SUBLANES = 8
CHUNK = 64
HG_SUB = 4
HG_LEVELS = (8, 16, 32, 64)
CONV_WIDTH = 4
LOG2_HEAD = HEAD.bit_length() - 1
LOG2_CHUNK = CHUNK.bit_length() - 1
LOG2_E = 1.4426950408889634
CHUNKS_PER_ITER = 2
V7X_VMEM_BYTES = 64 * 1024 * 1024


def _dot(a, b):
    return jnp.dot(a, b, preferred_element_type=F32)


def _dot_nt(a, b):
    return lax.dot_general(a, b, (((1,), (1,)), ((), ())), preferred_element_type=F32)


def _dot_tn(a, b):
    return lax.dot_general(a, b, (((0,), (0,)), ((), ())), preferred_element_type=F32)


def _mm(a, b):
    return _dot(a.astype(BF16), b.astype(BF16))


def _split2(x):
    hi = x.astype(BF16)
    lo = (x - hi.astype(F32)).astype(BF16)
    return hi, lo


def _sigmoid(x):
    return 1.0 / (1.0 + jnp.exp(-x))


def _silu(x):
    return x * _sigmoid(x)


def _softplus(x):
    return jnp.maximum(x, 0.0) + jnp.log1p(jnp.exp(-jnp.abs(x)))


def _iota(shape, dim):
    return lax.broadcasted_iota(jnp.int32, shape, dim)


def _head_sums(x, seg256):
    xb = x.astype(BF16)
    seg128 = seg256[0:LANES, 0:LANES]
    return jnp.concatenate([_dot(xb[:, 0:2 * LANES], seg256), _dot(xb[:, 2 * LANES:3 * LANES], seg128)], axis=1)


def _inproj_kernel(*refs, has_vres, rw_w, tiles_per_seq):
    it = iter(refs)
    x_ref, nw_ref, w_ref = (next(it) for _ in range(3))
    vfirst_in_ref = next(it) if has_vres else None
    mu_ref, w0_ref, a0_ref, wcode_ref = (next(it) for _ in range(4))
    if has_vres:
        v0_ref, vdn_ref, vup_ref = (next(it) for _ in range(3))
    kk_ref, ka_ref, rk_ref, tri2_ref, seg256_ref = (next(it) for _ in range(5))
    pack_ref, v_ref, bonus_ref, elast_ref, rest_ref = (next(it) for _ in range(5))
    vfirst_out_ref = None if has_vres else next(it)
    prev_ref = next(it)

    tm = x_ref.shape[0]
    sub = 2 * CHUNK
    n_sub = tm // sub
    rw_shift = 3 * rw_w + LANES
    rest_tiles = (w_ref.shape[1] - rw_shift) // LANES

    @pl.when(lax.rem(pl.program_id(0), tiles_per_seq) == 0)
    def _new_sequence():
        prev_ref[...] = jnp.zeros_like(prev_ref)

    x = x_ref[...]
    ms = jnp.mean(x * x, axis=-1, keepdims=True)
    hb = (x * lax.rsqrt(ms + NORM_EPS) * nw_ref[...]).astype(BF16)
    u_rw = _dot(hb, w_ref[:, 0:rw_shift])

    seg256 = seg256_ref[...]
    row = _iota((sub, 1), 0)
    lane_first_half = _iota((1, LANES), 1) < LANES // 2
    prev = prev_ref[...]
    for s in range(n_sub):
        lo = rw_shift + (rest_tiles * s // n_sub) * LANES
        hi = rw_shift + (rest_tiles * (s + 1) // n_sub) * LANES
        rest_ref[:, lo - rw_shift:hi - rw_shift] = _dot(hb, w_ref[:, lo:hi])

        rs = slice(s * sub, (s + 1) * sub)
        ru = u_rw[rs]
        sh = jnp.where(row == 0, prev, pltpu.roll(ru, 1, axis=0))
        prev = ru[sub - 1:sub, :]
        ul = ru + mu_ref[...] * (sh - ru)
        r = ul[:, 0:rw_w]
        k = ul[:, rw_w:2 * rw_w]
        v = ul[:, 2 * rw_w:3 * rw_w]
        codes = ul[:, 3 * rw_w:rw_shift]
        both = _mm(jnp.where(lane_first_half, jnp.tanh(codes), codes), wcode_ref[...])
        w_log = -_softplus(-(w0_ref[...] + both[:, 0:rw_w])) - 0.5
        nlw2 = jnp.exp(w_log) * LOG2_E
        a = _sigmoid(a0_ref[...] + both[:, rw_w:2 * rw_w])
        if has_vres:
            gate = _sigmoid(v0_ref[...] + _mm(_mm(v, vdn_ref[...]), vup_ref[...]))
            v = v + (vfirst_in_ref[rs, :] - v) * gate
        else:
            vfirst_out_ref[rs, :] = v
        kk = k * kk_ref[...]
        kk = kk / jnp.maximum(jnp.sqrt(_head_sums(kk * kk, seg256)), 1e-12)
        k2 = k * (1.0 + (a - 1.0) * ka_ref[...])
        bonus_ref[rs, :] = _head_sums(r * k2 * rk_ref[...], seg256) * v
        v_ref[rs, :] = v.astype(BF16)
        cs = _dot(tri2_ref[...], jnp.concatenate(_split2(nlw2), axis=0))
        e_pos = jnp.exp2(-cs)
        e_neg = jnp.exp2(cs)
        pack_ref[rs, 0:rw_w] = (-kk * jnp.exp2(nlw2 - cs)).astype(BF16)
        pack_ref[rs, rw_w:2 * rw_w] = (r * e_pos).astype(BF16)
        pack_ref[rs, 2 * rw_w:3 * rw_w] = (kk * a * e_neg).astype(BF16)
        pack_ref[rs, 3 * rw_w:4 * rw_w] = (k2 * e_neg).astype(BF16)
        for c in range(sub // CHUNK):
            n = s * (sub // CHUNK) + c
            elast_ref[n:n + 1, :] = e_pos[(c + 1) * CHUNK - 1:(c + 1) * CHUNK, :]
    prev_ref[...] = prev


def _row_block(n_rows, target):
    blk = min(target, n_rows)
    while n_rows % blk:
        blk //= 2
    return blk


def _vmem_limit(block_bytes):
    return int(min(V7X_VMEM_BYTES * 7 // 8, 2 * block_bytes + 16 * 1024 * 1024))


def _layer_spec(a, idx):
    nd = a.ndim
    return pl.BlockSpec((None,) + a.shape[1:], lambda *_: (idx,) + (0,) * (nd - 1))


def _inproj(x2, norm_w, w_bf16, v_first, p, seq):
    n, d = x2.shape
    n_in = w_bf16[0].shape[-1]
    rw_w = p["w0"][0].shape[-1]
    rw_shift = 3 * rw_w + LANES
    tm = _row_block(seq, 512)
    assert tm % (SUBLANES * CHUNK) == 0 and seq % tm == 0
    has_vres = v_first is not None

    def tok_spec(width):
        return pl.BlockSpec((tm, width), lambda i: (i, 0))

    def full_spec(a):
        nd = a.ndim
        return pl.BlockSpec(a.shape, lambda i: (0,) * nd)

    tri2, _, seg256 = _mixer_constants(2)
    args, in_specs = [x2], [tok_spec(d)]
    for a, idx in (norm_w, w_bf16):
        args.append(a)
        in_specs.append(_layer_spec(a, idx))
    if has_vres:
        args.append(v_first)
        in_specs.append(tok_spec(rw_w))
    names = ["mu", "w0", "a0", "wcode"] + (["v0", "vdn", "vup"] if has_vres else []) + ["kk", "ka", "rk"]
    for a, idx in [p[k] for k in names]:
        args.append(a)
        in_specs.append(_layer_spec(a, idx))
    for a in (tri2, seg256):
        args.append(a)
        in_specs.append(full_spec(a))
    out_shape = [jax.ShapeDtypeStruct((n, 4 * rw_w), BF16), jax.ShapeDtypeStruct((n, rw_w), BF16),
                 jax.ShapeDtypeStruct((n, rw_w), F32), jax.ShapeDtypeStruct((n // CHUNK, rw_w), F32),
                 jax.ShapeDtypeStruct((n, n_in - rw_shift), F32)]
    out_specs = [tok_spec(4 * rw_w), tok_spec(rw_w), tok_spec(rw_w),
                 pl.BlockSpec((tm // CHUNK, rw_w), lambda i: (i, 0)), tok_spec(n_in - rw_shift)]
    if not has_vres:
        out_shape.append(jax.ShapeDtypeStruct((n, rw_w), F32))
        out_specs.append(tok_spec(rw_w))
    block_bytes = tm * d * 4 + d * n_in * 2 + tm * (n_in + 4 * rw_w) * 4
    body = functools.partial(_inproj_kernel, has_vres=has_vres, rw_w=rw_w, tiles_per_seq=seq // tm)
    return pl.pallas_call(
        body,
        out_shape=out_shape,
        grid=(n // tm,),
        in_specs=in_specs,
        out_specs=out_specs,
        scratch_shapes=[pltpu.VMEM((1, rw_shift), F32)],
        compiler_params=pltpu.CompilerParams(
            dimension_semantics=("arbitrary",), vmem_limit_bytes=_vmem_limit(block_bytes)),
        name="inproj_vres" if has_vres else "inproj",
    )(*args)


def _stack_heads(x, lane_head0):
    return jnp.concatenate([jnp.where(lane_head0, x, 0.0), jnp.where(lane_head0, 0.0, x)], axis=0)


def _mix_kernel(*refs, layer, final, n_chunks, nb, rw_w, hg_w, lru_w):
    it = iter(refs)
    pack_ref, v_ref, bonus_ref, elast_ref, u_ref, x_ref, wout_ref = (next(it) for _ in range(7))
    fw_ref = next(it) if final else None
    lnw_ref, lnb_ref, lbraw_ref, hgnw_ref = (next(it) for _ in range(4))
    convw_ref, convb_ref, wa_ref, ba_ref, wx_ref, bx_ref, lam_ref = (next(it) for _ in range(7))
    hgm_ref, seg256_ref = (next(it) for _ in range(2))
    xout_ref = next(it)
    out_ref, rwstate_ref, hgstate_ref, lrux_ref, lruh_ref = (next(it) for _ in range(5))

    n_pairs = rw_w // LANES
    rows_all = nb * CHUNK
    o_rwz = 0
    o_hgq = o_rwz + rw_w
    o_hgf = o_hgq + hg_w
    o_hgi = o_hgf + hg_w
    o_hgz = o_hgi + hg_w
    o_lrx = o_hgz + hg_w
    o_lrz = o_lrx + lru_w

    @pl.when(pl.program_id(0) == 0)
    def _reset():
        rwstate_ref[...] = jnp.zeros_like(rwstate_ref)
        hgstate_ref[...] = jnp.zeros_like(hgstate_ref)
        lrux_ref[...] = jnp.zeros_like(lrux_ref)
        lruh_ref[...] = jnp.zeros_like(lruh_ref)

    row = _iota((rows_all, 1), 0)
    tpos = row & (CHUNK - 1)
    lane_head0 = _iota((1, LANES), 1) < HEAD
    ti = _iota((CHUNK, LANES), 0)
    tj = _iota((CHUNK, LANES), 1) & (CHUNK - 1)
    strict_lower = tj < ti
    incl_lower = tj <= ti
    eye = jnp.where(tj == ti, 1.0, 0.0)
    same_group = {size: (ti >> (size.bit_length() - 1)) == (tj >> (size.bit_length() - 1)) for size in HG_LEVELS}
    bd_state = (_iota((LANES, LANES), 0) >> LOG2_HEAD) == (_iota((LANES, LANES), 1) >> LOG2_HEAD)
    pairs = [(b, p) for p in range(n_pairs) for b in range(nb)]

    seg256 = seg256_ref[...]

    def segsum(x):
        return _head_sums(x, seg256)

    def stacked(m):
        return _stack_heads(m, lane_head0).astype(BF16)

    def per_batch(ref):
        val = ref[0]
        for b in range(1, nb):
            val = jnp.where(row >= b * CHUNK, ref[b], val)
        return val

    def tokens(ref, rows, lo, hi):
        return jnp.concatenate([ref[b, rows, lo:hi] for b in range(nb)], axis=0)

    def store(rows, lo, val):
        for b in range(nb):
            out_ref[b, rows, lo:lo + val.shape[1]] = val[b * CHUNK:(b + 1) * CHUNK].astype(out_ref.dtype)

    def blk(x, b, p):
        return x[b * CHUNK:(b + 1) * CHUNK, p * LANES:(p + 1) * LANES]

    lbraw = lbraw_ref[...]
    lbe = jnp.exp(lbraw - jnp.max(lbraw, axis=0, keepdims=True))
    lbw = lbe / jnp.sum(lbe, axis=0, keepdims=True)
    lb = lbw[0:1, :]
    for j in range(1, layer + 1):
        lb = lb + lbw[j:j + 1, :]
    lb = lb - lbw[0:1, :]

    def rwkv_stages(chunks):
        insts = [(ci, b, p) for ci in range(len(chunks)) for (b, p) in pairs]
        lhs, rhs, vs, a_ak, lrk, avs, pw, tinv = {}, {}, {}, {}, {}, {}, {}, {}
        for ci, (_, rows) in enumerate(chunks):
            a_t = tokens(pack_ref, rows, 0, rw_w)
            r_t = tokens(pack_ref, rows, rw_w, 2 * rw_w)
            b_t = tokens(pack_ref, rows, 2 * rw_w, 3 * rw_w)
            k_t = tokens(pack_ref, rows, 3 * rw_w, 4 * rw_w)
            v = tokens(v_ref, rows, 0, rw_w)
            for b, p in pairs:
                i = (ci, b, p)
                lhs[i] = jnp.concatenate([blk(a_t, b, p), blk(r_t, b, p)], axis=0)
                rhs[i] = jnp.concatenate([_stack_heads(blk(b_t, b, p), lane_head0),
                                          _stack_heads(blk(k_t, b, p), lane_head0)], axis=0)
                vs[i] = _stack_heads(blk(v, b, p), lane_head0)
        yield
        for i in insts:
            sc = _dot_nt(lhs[i], rhs[i])
            pw[i] = jnp.where(strict_lower, sc[0:CHUNK, 0:LANES], 0.0)
            a_ak[i] = jnp.where(strict_lower, sc[0:CHUNK, LANES:2 * LANES], 0.0).astype(BF16)
            lrk[i] = jnp.concatenate([jnp.where(incl_lower, sc[CHUNK:2 * CHUNK, 0:LANES], 0.0),
                                      jnp.where(incl_lower, sc[CHUNK:2 * CHUNK, LANES:2 * LANES], 0.0)],
                                     axis=1).astype(BF16)
        yield
        for i in insts:
            avs[i] = _dot(a_ak[i], vs[i])
            tinv[i] = eye + pw[i]
            pw[i] = _dot(pw[i].astype(BF16), stacked(pw[i]))
        yield
        for s in range(1, 5):
            for i in insts:
                both = _dot(jnp.concatenate([pw[i], tinv[i]], axis=0).astype(BF16), stacked(pw[i]))
                pw[i] = both[0:CHUNK]
                tinv[i] = tinv[i] + both[CHUNK:2 * CHUNK]
            yield
        for i in insts:
            tinv[i] = (tinv[i] + _dot(tinv[i].astype(BF16), stacked(pw[i]))).astype(BF16)
        yield
        for ci, (c, rows) in enumerate(chunks):
            st, am, uv, yb = {}, {}, {}, {}
            for n, (b, p) in enumerate(pairs):
                st[b, p] = rwstate_ref[n]
                am[b, p] = _dot_nt(lhs[ci, b, p], st[b, p].astype(BF16))
            yield
            for b, p in pairs:
                i = (ci, b, p)
                x = am[b, p][0:CHUNK] + avs[i]
                uv[b, p] = jnp.concatenate([stacked(_dot(tinv[i], stacked(x))), vs[i]], axis=0)
            yield
            for b, p in pairs:
                yb[b, p] = am[b, p][CHUNK:2 * CHUNK] + _dot(lrk[ci, b, p], uv[b, p])
            yield
            for n, (b, p) in enumerate(pairs):
                rwstate_ref[n] = ((st[b, p] + _dot_tn(uv[b, p], rhs[ci, b, p]))
                                  * elast_ref[b, c][:, p * LANES:(p + 1) * LANES])
            yield
            y = jnp.concatenate([jnp.concatenate([yb[b, p] for p in range(n_pairs)], axis=1) for b in range(nb)], axis=0)
            mean = segsum(y) * (1.0 / HEAD)
            yc = y - mean
            yield
            var = segsum(yc * yc) * (1.0 / HEAD)
            yn = yc * lax.rsqrt(var + RW_GN_EPS) * lnw_ref[...] + lnb_ref[...]
            store(rows, 0, (yn + tokens(bonus_ref, rows, 0, rw_w)) * _silu(tokens(u_ref, rows, o_rwz, o_rwz + rw_w)))
            yield

    def hgrn_stages(c, rows):
        q = tokens(u_ref, rows, o_hgq, o_hgq + hg_w)
        f = lb + (1.0 - lb) * _sigmoid(tokens(u_ref, rows, o_hgf, o_hgf + hg_w))
        g = jnp.log2(f)
        kx = 1.0 - f
        iv = tokens(u_ref, rows, o_hgi, o_hgi + hg_w)
        bb = _dot(hgm_ref[...], jnp.concatenate(_split2(g), axis=0))
        part = lambda n: bb[n * rows_all:(n + 1) * rows_all]
        bsub = part(0)
        n_lv = len(HG_LEVELS)
        yield
        bchunk = part(n_lv + 1)
        qc = (q * jnp.exp2(bchunk)).astype(BF16)
        kc = (kx * jnp.exp2(part(n_lv + 2))).astype(BF16)
        ivb = iv.astype(BF16)
        ocs = {}
        for n, (b, p) in enumerate(pairs):
            hst = hgstate_ref[n]
            ocs[b, p] = _dot_nt(blk(qc, b, p), hst.astype(BF16))
            upd = jnp.where(bd_state, _dot_tn(blk(ivb, b, p), blk(kc, b, p)), 0.0)
            last = (b + 1) * CHUNK - 1
            hgstate_ref[n] = hst * jnp.exp2(bchunk[last:last + 1, p * LANES:(p + 1) * LANES]) + upd
        yield
        scs = {i: None for i in pairs}
        for lv, size in enumerate(HG_LEVELS):
            pst = jnp.exp2(part(1 + lv))
            upper = (row & (size - 1)) >= size // 2
            qs = jnp.where(upper, q * pst, 0.0)
            ks = jnp.where(upper, 0.0, kx * pst)
            for i in pairs:
                sc = jnp.where(same_group[size], _dot_nt(blk(qs, *i).astype(BF16), stacked(blk(ks, *i))), 0.0)
                scs[i] = sc if scs[i] is None else scs[i] + sc
            yield
        ostr = {i: _mm(scs[i], stacked(blk(iv, *i))) + ocs[i] for i in pairs}
        yield
        sub = (rows_all // SUBLANES, SUBLANES, hg_w)
        pos = _iota((1, SUBLANES, 1), 1) & (HG_SUB - 1)
        q3, b3, k3, v3 = (z.reshape(sub) for z in (q, bsub, kx, iv))
        acc = segsum(q * kx) * iv
        for d in range(1, HG_SUB):
            wgt = jnp.where(pos >= d, q3 * jnp.exp2(b3 - pltpu.roll(b3, d, axis=1)) * pltpu.roll(k3, d, axis=1), 0.0)
            acc = acc + segsum(wgt.reshape(rows_all, hg_w)) * pltpu.roll(v3, d, axis=1).reshape(rows_all, hg_w)
            if d % 2 == 1:
                yield
        o = acc + jnp.concatenate(
            [jnp.concatenate([ostr[(b, p)] for p in range(n_pairs)], axis=1) for b in range(nb)], axis=0)
        ms = segsum(o * o) * (1.0 / HEAD)
        store(rows, rw_w, o * lax.rsqrt(ms + NORM_EPS) * hgnw_ref[...] * _silu(tokens(u_ref, rows, o_hgz, o_hgz + hg_w)))

    def lru_stages(c, rows):
        xb = tokens(u_ref, rows, o_lrx, o_lrx + lru_w)
        row8 = _iota((SUBLANES, 1), 0)
        yv = convb_ref[...] + convw_ref[CONV_WIDTH - 1:CONV_WIDTH, :] * xb
        for d in range(1, CONV_WIDTH):
            rolled = pltpu.roll(xb, d, axis=0)
            pieces = []
            for b in range(nb):
                tail = pltpu.roll(lrux_ref[b], d, axis=0)
                pieces.append(jnp.where(row8 < d, tail, rolled[b * CHUNK:b * CHUNK + SUBLANES]))
                pieces.append(rolled[b * CHUNK + SUBLANES:(b + 1) * CHUNK])
            yv = yv + convw_ref[CONV_WIDTH - 1 - d:CONV_WIDTH - d, :] * jnp.concatenate(pieces, axis=0)
        for b in range(nb):
            lrux_ref[b] = xb[(b + 1) * CHUNK - SUBLANES:(b + 1) * CHUNK]
        yield
        ybf = yv.astype(BF16)
        rg = _sigmoid(_dot(ybf, wa_ref[...]) + ba_ref[...])
        ig = _sigmoid(_dot(ybf, wx_ref[...]) + bx_ref[...])
        log_a = -LRU_C * rg * _softplus(-lam_ref[...])
        av = jnp.exp(log_a)
        th = jnp.tanh(log_a)
        gu = jnp.sqrt(-2.0 * th / (1.0 - th)) * (ig * yv)
        yield
        step = 1
        while step < CHUNK:
            keep = tpos >= step
            a_sh = jnp.where(keep, pltpu.roll(av, step, axis=0), 1.0)
            u_sh = jnp.where(keep, pltpu.roll(gu, step, axis=0), 0.0)
            gu = av * u_sh + gu
            av = av * a_sh
            step *= 2
            yield
        hv = gu + av * per_batch(lruh_ref)
        for b in range(nb):
            lruh_ref[b] = hv[(b + 1) * CHUNK - 1:(b + 1) * CHUNK]
        store(rows, rw_w + hg_w, hv * _silu(tokens(u_ref, rows, o_lrz, o_lrz + lru_w)))

    def project_rows(r0, n_rows):
        for b in range(nb):
            xn = x_ref[b, r0:r0 + n_rows, :] + _dot(out_ref[b, r0:r0 + n_rows, :], wout_ref[...])
            if final:
                ms = jnp.mean(xn * xn, axis=-1, keepdims=True)
                xn = xn * lax.rsqrt(ms + NORM_EPS) * fw_ref[...]
            xout_ref[b, r0:r0 + n_rows, :] = xn
            yield

    assert n_chunks % CHUNKS_PER_ITER == 0
    group_rows = CHUNKS_PER_ITER * CHUNK
    for gi in range(n_chunks // CHUNKS_PER_ITER):
        chunks = [(gi * CHUNKS_PER_ITER + j, pl.ds((gi * CHUNKS_PER_ITER + j) * CHUNK, CHUNK))
                  for j in range(CHUNKS_PER_ITER)]
        pending = [rwkv_stages(chunks)] + [hgrn_stages(*ch) for ch in chunks] + [lru_stages(*ch) for ch in chunks]
        if gi > 0:
            pending.append(project_rows((gi - 1) * group_rows, group_rows))
        while pending:
            for gen in list(pending):
                if next(gen, True):
                    pending.remove(gen)
    for _ in project_rows((n_chunks // CHUNKS_PER_ITER - 1) * group_rows, group_rows):
        pass


def _mixer_constants(nb):
    rows_all = nb * CHUNK
    t = np.arange(rows_all)
    same_chunk = (t[None, :] // CHUNK) == (t[:, None] // CHUNK)
    lower = t[None, :] <= t[:, None]
    tri2 = np.tile(lower & same_chunk, (1, 2))
    mats = [lower & ((t[None, :] // HG_SUB) == (t[:, None] // HG_SUB))]
    for size in HG_LEVELS:
        half = size // 2
        same = (t[None, :] // size) == (t[:, None] // size)
        upper_row = (t[:, None] % size) >= half
        upper_col = (t[None, :] % size) >= half
        mats.append(same & np.where(upper_row, upper_col & lower, (~upper_col) & (~lower)))
    mats.append(lower & same_chunk)
    mats.append((~lower) & same_chunk)
    hgm = np.tile(np.concatenate(mats, axis=0), (1, 2))
    h = np.arange(2 * LANES) // HEAD
    seg256 = h[None, :] == h[:, None]
    as_bf16 = lambda m: jnp.asarray(m.astype(np.float32), dtype=BF16)
    return as_bf16(tri2), as_bf16(hgm), as_bf16(seg256)


def _block_diag(w):
    nl, g, n, _ = w.shape
    out = jnp.zeros((nl, g * n, g * n), w.dtype)
    for i in range(g):
        out = out.at[:, i * n:(i + 1) * n, i * n:(i + 1) * n].set(w[:, i])
    return out


def _mixer(prep, x2, w_out_bf16, final_w, layer, p, batch, seq):
    pack, vb, bonus, elast, rest = prep
    d = x2.shape[1]
    rw_w = vb.shape[1]
    hg_w = p["hgnw"][0].shape[-1]
    lru_w = p["lam"][0].shape[-1]
    d_mix = rw_w + hg_w + lru_w
    tb = _row_block(seq, 256)
    assert tb % (CHUNKS_PER_ITER * CHUNK) == 0 and rw_w == 3 * LANES and hg_w == rw_w
    assert w_out_bf16[0].shape[-2] == d_mix

    def tok_spec(width):
        return pl.BlockSpec((batch, tb, width), lambda t: (0, t, 0))

    def full_spec(a):
        nd = a.ndim
        return pl.BlockSpec(a.shape, lambda t: (0,) * nd)

    per_token = lambda a: a.reshape(batch, seq, a.shape[1])
    args = [per_token(pack), per_token(vb), per_token(bonus), elast.reshape(batch, seq // CHUNK, 1, rw_w),
            per_token(rest), per_token(x2), w_out_bf16[0]]
    in_specs = [tok_spec(4 * rw_w), tok_spec(rw_w), tok_spec(rw_w),
                pl.BlockSpec((batch, tb // CHUNK, 1, rw_w), lambda t: (0, t, 0, 0)), tok_spec(rest.shape[1]),
                tok_spec(d), _layer_spec(*w_out_bf16)]
    if final_w is not None:
        args.append(final_w)
        in_specs.append(full_spec(final_w))
    names = ["lnw", "lnb", "lbraw", "hgnw", "convw", "convb", "wa", "ba", "wx", "bx", "lam"]
    for k in names:
        a, idx = p[k]
        args.append(a)
        in_specs.append(full_spec(a) if idx is None else _layer_spec(a, idx))
    _, hgm, seg256 = _mixer_constants(batch)
    for a in (hgm, seg256):
        args.append(a)
        in_specs.append(full_spec(a))

    n_pairs = rw_w // LANES
    scratch = [pltpu.VMEM((batch, tb, d_mix), BF16),
               pltpu.VMEM((batch * n_pairs, LANES, LANES), F32),
               pltpu.VMEM((batch * n_pairs, LANES, LANES), F32),
               pltpu.VMEM((batch, SUBLANES, lru_w), F32),
               pltpu.VMEM((batch, 1, lru_w), F32)]
    block_bytes = (batch * tb * (rest.shape[1] * 4 + 4 * rw_w * 2 + rw_w * 2 + rw_w * 4 + d_mix * 2 + 2 * d * 4)
                   + d_mix * d * 2)
    body = functools.partial(_mix_kernel, layer=layer, final=final_w is not None, n_chunks=tb // CHUNK, nb=batch,
                             rw_w=rw_w, hg_w=hg_w, lru_w=lru_w)
    x_new = pl.pallas_call(
        body,
        out_shape=jax.ShapeDtypeStruct((batch, seq, d), F32),
        grid=(seq // tb,),
        in_specs=in_specs,
        out_specs=tok_spec(d),
        scratch_shapes=scratch,
        compiler_params=pltpu.CompilerParams(
            dimension_semantics=("arbitrary",), vmem_limit_bytes=_vmem_limit(block_bytes)),
        name=f"mixer_l{layer}",
    )(*args)
    return x_new.reshape(batch * seq, d)


def kernel(x, norm_w, w_in, rw_mu, rw_w0, rw_w_up, rw_a0, rw_a_up, rw_v0, rw_v_dn, rw_v_up, rw_k_k, rw_k_a, rw_r_k, rw_ln_w, rw_ln_b, hg_lb_raw, hg_norm_w, lru_conv_w, lru_conv_b, lru_wa, lru_ba, lru_wx, lru_bx, lru_lambda, w_out, final_norm_w):
    batch, seq, d = x.shape
    depth = w_in.shape[0]
    rw_w = rw_w0.shape[1]
    lora = rw_w_up.shape[1]
    assert 2 * lora == LANES
    x2 = x.reshape(batch * seq, d)
    zeros_code = jnp.zeros((depth, lora, rw_w), F32)
    per_layer = lambda a: a.reshape(a.shape[0], 1, -1)
    stacked = {
        "norm": per_layer(norm_w), "win": w_in.astype(BF16), "wout": w_out.astype(BF16),
        "mu": per_layer(rw_mu), "w0": per_layer(rw_w0), "a0": per_layer(rw_a0),
        "wcode": jnp.concatenate([jnp.concatenate([rw_w_up, zeros_code], axis=2),
                                  jnp.concatenate([zeros_code, rw_a_up], axis=2)], axis=1).astype(BF16),
        "kk": per_layer(rw_k_k), "ka": per_layer(rw_k_a), "rk": per_layer(rw_r_k),
        "lnw": per_layer(rw_ln_w), "lnb": per_layer(rw_ln_b), "hgnw": per_layer(hg_norm_w),
        "convw": lru_conv_w, "convb": per_layer(lru_conv_b),
        "wa": _block_diag(lru_wa).astype(BF16), "ba": per_layer(lru_ba),
        "wx": _block_diag(lru_wx).astype(BF16), "bx": per_layer(lru_bx), "lam": per_layer(lru_lambda),
    }
    vres = {"v0": per_layer(rw_v0), "vdn": rw_v_dn.astype(BF16), "vup": rw_v_up.astype(BF16)}
    v_first = None
    for l in range(depth):
        p = {k: (a, l) for k, a in stacked.items()}
        p["lbraw"] = (hg_lb_raw, None)
        if l > 0:
            p.update({k: (a, l - 1) for k, a in vres.items()})
        prep = _inproj(x2, p["norm"], p["win"], v_first, p, seq)
        if l == 0:
            v_first = prep[5]
        fw = final_norm_w.reshape(1, -1) if l == depth - 1 else None
        x2 = _mixer(prep[:5], x2, p["wout"], fw, l, p, batch, seq)
    return x2.reshape(batch, seq, d)
```

```python
import functools

import numpy as np
import jax
import jax.numpy as jnp
from jax import lax
from jax.experimental import pallas as pl
from jax.experimental.pallas import tpu as pltpu

F32 = jnp.float32
BF16 = jnp.bfloat16

NORM_EPS = 1e-6
RW_GN_EPS = 64e-5
LRU_C = 8.0
HEAD = 64
LANES = 128
SUBLANES = 8
CHUNK = 64
HG_SUB = 4
HG_LEVELS = (8, 16, 32, 64)
CONV_WIDTH = 4
LOG2_HEAD = HEAD.bit_length() - 1
LOG2_CHUNK = CHUNK.bit_length() - 1
LOG2_E = 1.4426950408889634
CHUNKS_PER_ITER = 2
V7X_VMEM_BYTES = 64 * 1024 * 1024


def _dot(a, b):
    return jnp.dot(a, b, preferred_element_type=F32)


def _dot_nt(a, b):
    return lax.dot_general(a, b, (((1,), (1,)), ((), ())), preferred_element_type=F32)


def _dot_tn(a, b):
    return lax.dot_general(a, b, (((0,), (0,)), ((), ())), preferred_element_type=F32)


def _mm(a, b):
    return _dot(a.astype(BF16), b.astype(BF16))


def _split2(x):
    hi = x.astype(BF16)
    lo = (x - hi.astype(F32)).astype(BF16)
    return hi, lo


def _sigmoid(x):
    return 1.0 / (1.0 + jnp.exp(-x))


def _silu(x):
    return x * _sigmoid(x)


def _softplus(x):
    return jnp.maximum(x, 0.0) + jnp.log1p(jnp.exp(-jnp.abs(x)))


def _iota(shape, dim):
    return lax.broadcasted_iota(jnp.int32, shape, dim)


def _head_sums(x, seg256):
    xb = x.astype(BF16)
    seg128 = seg256[0:LANES, 0:LANES]
    return jnp.concatenate([_dot(xb[:, 0:2 * LANES], seg256), _dot(xb[:, 2 * LANES:3 * LANES], seg128)], axis=1)


def _inproj_kernel(*refs, has_vres, rw_w, tiles_per_seq):
    it = iter(refs)
    x_ref, nw_ref, w_ref = (next(it) for _ in range(3))
    vfirst_in_ref = next(it) if has_vres else None
    mu_ref, w0_ref, a0_ref, wcode_ref = (next(it) for _ in range(4))
    if has_vres:
        v0_ref, vdn_ref, vup_ref = (next(it) for _ in range(3))
    kk_ref, ka_ref, rk_ref, tri2_ref, seg256_ref = (next(it) for _ in range(5))
    pack_ref, v_ref, bonus_ref, elast_ref, rest_ref = (next(it) for _ in range(5))
    vfirst_out_ref = None if has_vres else next(it)
    prev_ref = next(it)

    tm = x_ref.shape[0]
    sub = 2 * CHUNK
    n_sub = tm // sub
    rw_shift = 3 * rw_w + LANES
    rest_tiles = (w_ref.shape[1] - rw_shift) // LANES

    @pl.when(lax.rem(pl.program_id(0), tiles_per_seq) == 0)
    def _new_sequence():
        prev_ref[...] = jnp.zeros_like(prev_ref)

    x = x_ref[...]
    ms = jnp.mean(x * x, axis=-1, keepdims=True)
    hb = (x * lax.rsqrt(ms + NORM_EPS) * nw_ref[...]).astype(BF16)
    u_rw = _dot(hb, w_ref[:, 0:rw_shift])

    seg256 = seg256_ref[...]
    row = _iota((sub, 1), 0)
    lane_first_half = _iota((1, LANES), 1) < LANES // 2
    prev = prev_ref[...]
    for s in range(n_sub):
        lo = rw_shift + (rest_tiles * s // n_sub) * LANES
        hi = rw_shift + (rest_tiles * (s + 1) // n_sub) * LANES
        rest_ref[:, lo - rw_shift:hi - rw_shift] = _dot(hb, w_ref[:, lo:hi])

        rs = slice(s * sub, (s + 1) * sub)
        ru = u_rw[rs]
        sh = jnp.where(row == 0, prev, pltpu.roll(ru, 1, axis=0))
        prev = ru[sub - 1:sub, :]
        ul = ru + mu_ref[...] * (sh - ru)
        r = ul[:, 0:rw_w]
        k = ul[:, rw_w:2 * rw_w]
        v = ul[:, 2 * rw_w:3 * rw_w]
        codes = ul[:, 3 * rw_w:rw_shift]
        both = _mm(jnp.where(lane_first_half, jnp.tanh(codes), codes), wcode_ref[...])
        w_log = -_softplus(-(w0_ref[...] + both[:, 0:rw_w])) - 0.5
        nlw2 = jnp.exp(w_log) * LOG2_E
        a = _sigmoid(a0_ref[...] + both[:, rw_w:2 * rw_w])
        if has_vres:
            gate = _sigmoid(v0_ref[...] + _mm(_mm(v, vdn_ref[...]), vup_ref[...]))
            v = v + (vfirst_in_ref[rs, :] - v) * gate
        else:
            vfirst_out_ref[rs, :] = v
        kk = k * kk_ref[...]
        kk = kk / jnp.maximum(jnp.sqrt(_head_sums(kk * kk, seg256)), 1e-12)
        k2 = k * (1.0 + (a - 1.0) * ka_ref[...])
        bonus_ref[rs, :] = _head_sums(r * k2 * rk_ref[...], seg256) * v
        v_ref[rs, :] = v.astype(BF16)
        cs = _dot(tri2_ref[...], jnp.concatenate(_split2(nlw2), axis=0))
        e_pos = jnp.exp2(-cs)
        e_neg = jnp.exp2(cs)
        pack_ref[rs, 0:rw_w] = (-kk * jnp.exp2(nlw2 - cs)).astype(BF16)
        pack_ref[rs, rw_w:2 * rw_w] = (r * e_pos).astype(BF16)
        pack_ref[rs, 2 * rw_w:3 * rw_w] = (kk * a * e_neg).astype(BF16)
        pack_ref[rs, 3 * rw_w:4 * rw_w] = (k2 * e_neg).astype(BF16)
        for c in range(sub // CHUNK):
            n = s * (sub // CHUNK) + c
            elast_ref[n:n + 1, :] = e_pos[(c + 1) * CHUNK - 1:(c + 1) * CHUNK, :]
    prev_ref[...] = prev


def _row_block(n_rows, target):
    blk = min(target, n_rows)
    while n_rows % blk:
        blk //= 2
    return blk


def _vmem_limit(block_bytes):
    return int(min(V7X_VMEM_BYTES * 7 // 8, 2 * block_bytes + 16 * 1024 * 1024))


def _layer_spec(a, idx):
    nd = a.ndim
    return pl.BlockSpec((None,) + a.shape[1:], lambda *_: (idx,) + (0,) * (nd - 1))


def _inproj(x2, norm_w, w_bf16, v_first, p, seq):
    n, d = x2.shape
    n_in = w_bf16[0].shape[-1]
    rw_w = p["w0"][0].shape[-1]
    rw_shift = 3 * rw_w + LANES
    tm = _row_block(seq, 512)
    assert tm % (SUBLANES * CHUNK) == 0 and seq % tm == 0
    has_vres = v_first is not None

    def tok_spec(width):
        return pl.BlockSpec((tm, width), lambda i: (i, 0))

    def full_spec(a):
        nd = a.ndim
        return pl.BlockSpec(a.shape, lambda i: (0,) * nd)

    tri2, _, seg256 = _mixer_constants(2)
    args, in_specs = [x2], [tok_spec(d)]
    for a, idx in (norm_w, w_bf16):
        args.append(a)
        in_specs.append(_layer_spec(a, idx))
    if has_vres:
        args.append(v_first)
        in_specs.append(tok_spec(rw_w))
    names = ["mu", "w0", "a0", "wcode"] + (["v0", "vdn", "vup"] if has_vres else []) + ["kk", "ka", "rk"]
    for a, idx in [p[k] for k in names]:
        args.append(a)
        in_specs.append(_layer_spec(a, idx))
    for a in (tri2, seg256):
        args.append(a)
        in_specs.append(full_spec(a))
    out_shape = [jax.ShapeDtypeStruct((n, 4 * rw_w), BF16), jax.ShapeDtypeStruct((n, rw_w), BF16),
                 jax.ShapeDtypeStruct((n, rw_w), F32), jax.ShapeDtypeStruct((n // CHUNK, rw_w), F32),
                 jax.ShapeDtypeStruct((n, n_in - rw_shift), F32)]
    out_specs = [tok_spec(4 * rw_w), tok_spec(rw_w), tok_spec(rw_w),
                 pl.BlockSpec((tm // CHUNK, rw_w), lambda i: (i, 0)), tok_spec(n_in - rw_shift)]
    if not has_vres:
        out_shape.append(jax.ShapeDtypeStruct((n, rw_w), F32))
        out_specs.append(tok_spec(rw_w))
    block_bytes = tm * d * 4 + d * n_in * 2 + tm * (n_in + 4 * rw_w) * 4
    body = functools.partial(_inproj_kernel, has_vres=has_vres, rw_w=rw_w, tiles_per_seq=seq // tm)
    return pl.pallas_call(
        body,
        out_shape=out_shape,
        grid=(n // tm,),
        in_specs=in_specs,
        out_specs=out_specs,
        scratch_shapes=[pltpu.VMEM((1, rw_shift), F32)],
        compiler_params=pltpu.CompilerParams(
            dimension_semantics=("arbitrary",), vmem_limit_bytes=_vmem_limit(block_bytes)),
        name="inproj_vres" if has_vres else "inproj",
    )(*args)


def _stack_heads(x, lane_head0):
    return jnp.concatenate([jnp.where(lane_head0, x, 0.0), jnp.where(lane_head0, 0.0, x)], axis=0)


def _mix_kernel(*refs, layer, final, n_chunks, nb, rw_w, hg_w, lru_w):
    it = iter(refs)
    pack_ref, v_ref, bonus_ref, elast_ref, u_ref, x_ref, wout_ref = (next(it) for _ in range(7))
    fw_ref = next(it) if final else None
    lnw_ref, lnb_ref, lbraw_ref, hgnw_ref = (next(it) for _ in range(4))
    convw_ref, convb_ref, wa_ref, ba_ref, wx_ref, bx_ref, lam_ref = (next(it) for _ in range(7))
    hgm_ref, seg256_ref = (next(it) for _ in range(2))
    xout_ref = next(it)
    out_ref, rwstate_ref, hgstate_ref, lrux_ref, lruh_ref = (next(it) for _ in range(5))

    n_pairs = rw_w // LANES
    rows_all = nb * CHUNK
    o_rwz = 0
    o_hgq = o_rwz + rw_w
    o_hgf = o_hgq + hg_w
    o_hgi = o_hgf + hg_w
    o_hgz = o_hgi + hg_w
    o_lrx = o_hgz + hg_w
    o_lrz = o_lrx + lru_w

    @pl.when(pl.program_id(0) == 0)
    def _reset():
        rwstate_ref[...] = jnp.zeros_like(rwstate_ref)
        hgstate_ref[...] = jnp.zeros_like(hgstate_ref)
        lrux_ref[...] = jnp.zeros_like(lrux_ref)
        lruh_ref[...] = jnp.zeros_like(lruh_ref)

    row = _iota((rows_all, 1), 0)
    tpos = row & (CHUNK - 1)
    lane_head0 = _iota((1, LANES), 1) < HEAD
    ti = _iota((CHUNK, LANES), 0)
    tj = _iota((CHUNK, LANES), 1) & (CHUNK - 1)
    strict_lower = tj < ti
    incl_lower = tj <= ti
    eye = jnp.where(tj == ti, 1.0, 0.0)
    same_group = {size: (ti >> (size.bit_length() - 1)) == (tj >> (size.bit_length() - 1)) for size in HG_LEVELS}
    bd_state = (_iota((LANES, LANES), 0) >> LOG2_HEAD) == (_iota((LANES, LANES), 1) >> LOG2_HEAD)
    pairs = [(b, p) for p in range(n_pairs) for b in range(nb)]

    seg256 = seg256_ref[...]

    def segsum(x):
        return _head_sums(x, seg256)

    def stacked(m):
        return _stack_heads(m, lane_head0).astype(BF16)

    def per_batch(ref):
        val = ref[0]
        for b in range(1, nb):
            val = jnp.where(row >= b * CHUNK, ref[b], val)
        return val

    def tokens(ref, rows, lo, hi):
        return jnp.concatenate([ref[b, rows, lo:hi] for b in range(nb)], axis=0)

    def store(rows, lo, val):
        for b in range(nb):
            out_ref[b, rows, lo:lo + val.shape[1]] = val[b * CHUNK:(b + 1) * CHUNK].astype(out_ref.dtype)

    def blk(x, b, p):
        return x[b * CHUNK:(b + 1) * CHUNK, p * LANES:(p + 1) * LANES]

    lbraw = lbraw_ref[...]
    lbe = jnp.exp(lbraw - jnp.max(lbraw, axis=0, keepdims=True))
    lbw = lbe / jnp.sum(lbe, axis=0, keepdims=True)
    lb = lbw[0:1, :]
    for j in range(1, layer + 1):
        lb = lb + lbw[j:j + 1, :]
    lb = lb - lbw[0:1, :]

    def rwkv_stages(chunks):
        insts = [(ci, b, p) for ci in range(len(chunks)) for (b, p) in pairs]
        lhs, rhs, vs, a_ak, lrk, avs, pw, tinv = {}, {}, {}, {}, {}, {}, {}, {}
        for ci, (_, rows) in enumerate(chunks):
            a_t = tokens(pack_ref, rows, 0, rw_w)
            r_t = tokens(pack_ref, rows, rw_w, 2 * rw_w)
            b_t = tokens(pack_ref, rows, 2 * rw_w, 3 * rw_w)
            k_t = tokens(pack_ref, rows, 3 * rw_w, 4 * rw_w)
            v = tokens(v_ref, rows, 0, rw_w)
            for b, p in pairs:
                i = (ci, b, p)
                lhs[i] = jnp.concatenate([blk(a_t, b, p), blk(r_t, b, p)], axis=0)
                rhs[i] = jnp.concatenate([_stack_heads(blk(b_t, b, p), lane_head0),
                                          _stack_heads(blk(k_t, b, p), lane_head0)], axis=0)
                vs[i] = _stack_heads(blk(v, b, p), lane_head0)
        yield
        for i in insts:
            sc = _dot_nt(lhs[i], rhs[i])
            pw[i] = jnp.where(strict_lower, sc[0:CHUNK, 0:LANES], 0.0)
            a_ak[i] = jnp.where(strict_lower, sc[0:CHUNK, LANES:2 * LANES], 0.0).astype(BF16)
            lrk[i] = jnp.concatenate([jnp.where(incl_lower, sc[CHUNK:2 * CHUNK, 0:LANES], 0.0),
                                      jnp.where(incl_lower, sc[CHUNK:2 * CHUNK, LANES:2 * LANES], 0.0)],
                                     axis=1).astype(BF16)
        yield
        for i in insts:
            avs[i] = _dot(a_ak[i], vs[i])
            tinv[i] = eye + pw[i]
            pw[i] = _dot(pw[i].astype(BF16), stacked(pw[i]))
        yield
        for s in range(1, 5):
            for i in insts:
                both = _dot(jnp.concatenate([pw[i], tinv[i]], axis=0).astype(BF16), stacked(pw[i]))
                pw[i] = both[0:CHUNK]
                tinv[i] = tinv[i] + both[CHUNK:2 * CHUNK]
            yield
        for i in insts:
            tinv[i] = (tinv[i] + _dot(tinv[i].astype(BF16), stacked(pw[i]))).astype(BF16)
        yield
        for ci, (c, rows) in enumerate(chunks):
            st, am, uv, yb = {}, {}, {}, {}
            for n, (b, p) in enumerate(pairs):
                st[b, p] = rwstate_ref[n]
                am[b, p] = _dot_nt(lhs[ci, b, p], st[b, p].astype(BF16))
            yield
            for b, p in pairs:
                i = (ci, b, p)
                x = am[b, p][0:CHUNK] + avs[i]
                uv[b, p] = jnp.concatenate([stacked(_dot(tinv[i], stacked(x))), vs[i]], axis=0)
            yield
            for b, p in pairs:
                yb[b, p] = am[b, p][CHUNK:2 * CHUNK] + _dot(lrk[ci, b, p], uv[b, p])
            yield
            for n, (b, p) in enumerate(pairs):
                rwstate_ref[n] = ((st[b, p] + _dot_tn(uv[b, p], rhs[ci, b, p]))
                                  * elast_ref[b, c][:, p * LANES:(p + 1) * LANES])
            yield
            y = jnp.concatenate([jnp.concatenate([yb[b, p] for p in range(n_pairs)], axis=1) for b in range(nb)], axis=0)
            mean = segsum(y) * (1.0 / HEAD)
            yc = y - mean
            yield
            var = segsum(yc * yc) * (1.0 / HEAD)
            yn = yc * lax.rsqrt(var + RW_GN_EPS) * lnw_ref[...] + lnb_ref[...]
            store(rows, 0, (yn + tokens(bonus_ref, rows, 0, rw_w)) * _silu(tokens(u_ref, rows, o_rwz, o_rwz + rw_w)))
            yield

    def hgrn_stages(c, rows):
        q = tokens(u_ref, rows, o_hgq, o_hgq + hg_w)
        f = lb + (1.0 - lb) * _sigmoid(tokens(u_ref, rows, o_hgf, o_hgf + hg_w))
        g = jnp.log2(f)
        kx = 1.0 - f
        iv = tokens(u_ref, rows, o_hgi, o_hgi + hg_w)
        bb = _dot(hgm_ref[...], jnp.concatenate(_split2(g), axis=0))
        part = lambda n: bb[n * rows_all:(n + 1) * rows_all]
        bsub = part(0)
        n_lv = len(HG_LEVELS)
        yield
        bchunk = part(n_lv + 1)
        qc = (q * jnp.exp2(bchunk)).astype(BF16)
        kc = (kx * jnp.exp2(part(n_lv + 2))).astype(BF16)
        ivb = iv.astype(BF16)
        ocs = {}
        for n, (b, p) in enumerate(pairs):
            hst = hgstate_ref[n]
            ocs[b, p] = _dot_nt(blk(qc, b, p), hst.astype(BF16))
            upd = jnp.where(bd_state, _dot_tn(blk(ivb, b, p), blk(kc, b, p)), 0.0)
            last = (b + 1) * CHUNK - 1
            hgstate_ref[n] = hst * jnp.exp2(bchunk[last:last + 1, p * LANES:(p + 1) * LANES]) + upd
        yield
        scs = {i: None for i in pairs}
        for lv, size in enumerate(HG_LEVELS):
            pst = jnp.exp2(part(1 + lv))
            upper = (row & (size - 1)) >= size // 2
            qs = jnp.where(upper, q * pst, 0.0)
            ks = jnp.where(upper, 0.0, kx * pst)
            for i in pairs:
                sc = jnp.where(same_group[size], _dot_nt(blk(qs, *i).astype(BF16), stacked(blk(ks, *i))), 0.0)
                scs[i] = sc if scs[i] is None else scs[i] + sc
            yield
        ostr = {i: _mm(scs[i], stacked(blk(iv, *i))) + ocs[i] for i in pairs}
        yield
        sub = (rows_all // SUBLANES, SUBLANES, hg_w)
        pos = _iota((1, SUBLANES, 1), 1) & (HG_SUB - 1)
        q3, b3, k3, v3 = (z.reshape(sub) for z in (q, bsub, kx, iv))
        wgts = [q * kx]
        for d in range(1, HG_SUB):
            wgt = jnp.where(pos >= d, q3 * jnp.exp2(b3 - pltpu.roll(b3, d, axis=1)) * pltpu.roll(k3, d, axis=1), 0.0)
            wgts.append(wgt.reshape(rows_all, hg_w))
        yield
        sums = segsum(jnp.concatenate(wgts, axis=0))
        yield
        acc = sums[0:rows_all] * iv
        for d in range(1, HG_SUB):
            acc = acc + sums[d * rows_all:(d + 1) * rows_all] * pltpu.roll(v3, d, axis=1).reshape(rows_all, hg_w)
        o = acc + jnp.concatenate(
            [jnp.concatenate([ostr[(b, p)] for p in range(n_pairs)], axis=1) for b in range(nb)], axis=0)
        ms = segsum(o * o) * (1.0 / HEAD)
        store(rows, rw_w, o * lax.rsqrt(ms + NORM_EPS) * hgnw_ref[...] * _silu(tokens(u_ref, rows, o_hgz, o_hgz + hg_w)))

    def lru_stages(c, rows):
        xb = tokens(u_ref, rows, o_lrx, o_lrx + lru_w)
        row8 = _iota((SUBLANES, 1), 0)
        yv = convb_ref[...] + convw_ref[CONV_WIDTH - 1:CONV_WIDTH, :] * xb
        for d in range(1, CONV_WIDTH):
            rolled = pltpu.roll(xb, d, axis=0)
            pieces = []
            for b in range(nb):
                tail = pltpu.roll(lrux_ref[b], d, axis=0)
                pieces.append(jnp.where(row8 < d, tail, rolled[b * CHUNK:b * CHUNK + SUBLANES]))
                pieces.append(rolled[b * CHUNK + SUBLANES:(b + 1) * CHUNK])
            yv = yv + convw_ref[CONV_WIDTH - 1 - d:CONV_WIDTH - d, :] * jnp.concatenate(pieces, axis=0)
        for b in range(nb):
            lrux_ref[b] = xb[(b + 1) * CHUNK - SUBLANES:(b + 1) * CHUNK]
        yield
        ybf = yv.astype(BF16)
        rg = _sigmoid(_dot(ybf, wa_ref[...]) + ba_ref[...])
        ig = _sigmoid(_dot(ybf, wx_ref[...]) + bx_ref[...])
        log_a = -LRU_C * rg * _softplus(-lam_ref[...])
        av = jnp.exp(log_a)
        th = jnp.tanh(log_a)
        gu = jnp.sqrt(-2.0 * th / (1.0 - th)) * (ig * yv)
        yield
        step = 1
        while step < CHUNK:
            keep = tpos >= step
            a_sh = jnp.where(keep, pltpu.roll(av, step, axis=0), 1.0)
            u_sh = jnp.where(keep, pltpu.roll(gu, step, axis=0), 0.0)
            gu = av * u_sh + gu
            av = av * a_sh
            step *= 2
            yield
        hv = gu + av * per_batch(lruh_ref)
        for b in range(nb):
            lruh_ref[b] = hv[(b + 1) * CHUNK - 1:(b + 1) * CHUNK]
        store(rows, rw_w + hg_w, hv * _silu(tokens(u_ref, rows, o_lrz, o_lrz + lru_w)))

    def project_rows(r0, n_rows):
        mixed = jnp.concatenate([out_ref[b, r0:r0 + n_rows, :] for b in range(nb)], axis=0)
        proj = _dot(mixed, wout_ref[...])
        yield
        for b in range(nb):
            xn = x_ref[b, r0:r0 + n_rows, :] + proj[b * n_rows:(b + 1) * n_rows]
            if final:
                ms = jnp.mean(xn * xn, axis=-1, keepdims=True)
                xn = xn * lax.rsqrt(ms + NORM_EPS) * fw_ref[...]
            xout_ref[b, r0:r0 + n_rows, :] = xn
        yield

    assert n_chunks % CHUNKS_PER_ITER == 0
    group_rows = CHUNKS_PER_ITER * CHUNK
    for gi in range(n_chunks // CHUNKS_PER_ITER):
        chunks = [(gi * CHUNKS_PER_ITER + j, pl.ds((gi * CHUNKS_PER_ITER + j) * CHUNK, CHUNK))
                  for j in range(CHUNKS_PER_ITER)]
        pending = [rwkv_stages(chunks)] + [hgrn_stages(*ch) for ch in chunks] + [lru_stages(*ch) for ch in chunks]
        if gi > 0:
            pending.append(project_rows((gi - 1) * group_rows, group_rows))
        while pending:
            for gen in list(pending):
                if next(gen, True):
                    pending.remove(gen)
    for _ in project_rows((n_chunks // CHUNKS_PER_ITER - 1) * group_rows, group_rows):
        pass


def _mixer_constants(nb):
    rows_all = nb * CHUNK
    t = np.arange(rows_all)
    same_chunk = (t[None, :] // CHUNK) == (t[:, None] // CHUNK)
    lower = t[None, :] <= t[:, None]
    tri2 = np.tile(lower & same_chunk, (1, 2))
    mats = [lower & ((t[None, :] // HG_SUB) == (t[:, None] // HG_SUB))]
    for size in HG_LEVELS:
        half = size // 2
        same = (t[None, :] // size) == (t[:, None] // size)
        upper_row = (t[:, None] % size) >= half
        upper_col = (t[None, :] % size) >= half
        mats.append(same & np.where(upper_row, upper_col & lower, (~upper_col) & (~lower)))
    mats.append(lower & same_chunk)
    mats.append((~lower) & same_chunk)
    hgm = np.tile(np.concatenate(mats, axis=0), (1, 2))
    h = np.arange(2 * LANES) // HEAD
    seg256 = h[None, :] == h[:, None]
    as_bf16 = lambda m: jnp.asarray(m.astype(np.float32), dtype=BF16)
    return as_bf16(tri2), as_bf16(hgm), as_bf16(seg256)


def _block_diag(w):
    nl, g, n, _ = w.shape
    out = jnp.zeros((nl, g * n, g * n), w.dtype)
    for i in range(g):
        out = out.at[:, i * n:(i + 1) * n, i * n:(i + 1) * n].set(w[:, i])
    return out


def _mixer(prep, x2, w_out_bf16, final_w, layer, p, batch, seq):
    pack, vb, bonus, elast, rest = prep
    d = x2.shape[1]
    rw_w = vb.shape[1]
    hg_w = p["hgnw"][0].shape[-1]
    lru_w = p["lam"][0].shape[-1]
    d_mix = rw_w + hg_w + lru_w
    tb = _row_block(seq, 256)
    assert tb % (CHUNKS_PER_ITER * CHUNK) == 0 and rw_w == 3 * LANES and hg_w == rw_w
    assert w_out_bf16[0].shape[-2] == d_mix

    def tok_spec(width):
        return pl.BlockSpec((batch, tb, width), lambda t: (0, t, 0))

    def full_spec(a):
        nd = a.ndim
        return pl.BlockSpec(a.shape, lambda t: (0,) * nd)

    per_token = lambda a: a.reshape(batch, seq, a.shape[1])
    args = [per_token(pack), per_token(vb), per_token(bonus), elast.reshape(batch, seq // CHUNK, 1, rw_w),
            per_token(rest), per_token(x2), w_out_bf16[0]]
    in_specs = [tok_spec(4 * rw_w), tok_spec(rw_w), tok_spec(rw_w),
                pl.BlockSpec((batch, tb // CHUNK, 1, rw_w), lambda t: (0, t, 0, 0)), tok_spec(rest.shape[1]),
                tok_spec(d), _layer_spec(*w_out_bf16)]
    if final_w is not None:
        args.append(final_w)
        in_specs.append(full_spec(final_w))
    names = ["lnw", "lnb", "lbraw", "hgnw", "convw", "convb", "wa", "ba", "wx", "bx", "lam"]
    for k in names:
        a, idx = p[k]
        args.append(a)
        in_specs.append(full_spec(a) if idx is None else _layer_spec(a, idx))
    _, hgm, seg256 = _mixer_constants(batch)
    for a in (hgm, seg256):
        args.append(a)
        in_specs.append(full_spec(a))

    n_pairs = rw_w // LANES
    scratch = [pltpu.VMEM((batch, tb, d_mix), BF16),
               pltpu.VMEM((batch * n_pairs, LANES, LANES), F32),
               pltpu.VMEM((batch * n_pairs, LANES, LANES), F32),
               pltpu.VMEM((batch, SUBLANES, lru_w), F32),
               pltpu.VMEM((batch, 1, lru_w), F32)]
    block_bytes = (batch * tb * (rest.shape[1] * 4 + 4 * rw_w * 2 + rw_w * 2 + rw_w * 4 + d_mix * 2 + 2 * d * 4)
                   + d_mix * d * 2)
    body = functools.partial(_mix_kernel, layer=layer, final=final_w is not None, n_chunks=tb // CHUNK, nb=batch,
                             rw_w=rw_w, hg_w=hg_w, lru_w=lru_w)
    x_new = pl.pallas_call(
        body,
        out_shape=jax.ShapeDtypeStruct((batch, seq, d), F32),
        grid=(seq // tb,),
        in_specs=in_specs,
        out_specs=tok_spec(d),
        scratch_shapes=scratch,
        compiler_params=pltpu.CompilerParams(
            dimension_semantics=("arbitrary",), vmem_limit_bytes=_vmem_limit(block_bytes)),
        name=f"mixer_l{layer}",
    )(*args)
    return x_new.reshape(batch * seq, d)


def kernel(x, norm_w, w_in, rw_mu, rw_w0, rw_w_up, rw_a0, rw_a_up, rw_v0, rw_v_dn, rw_v_up, rw_k_k, rw_k_a, rw_r_k, rw_ln_w, rw_ln_b, hg_lb_raw, hg_norm_w, lru_conv_w, lru_conv_b, lru_wa, lru_ba, lru_wx, lru_bx, lru_lambda, w_out, final_norm_w):
    batch, seq, d = x.shape
    depth = w_in.shape[0]
    rw_w = rw_w0.shape[1]
    lora = rw_w_up.shape[1]
    assert 2 * lora == LANES
    x2 = x.reshape(batch * seq, d)
    zeros_code = jnp.zeros((depth, lora, rw_w), F32)
    per_layer = lambda a: a.reshape(a.shape[0], 1, -1)
    stacked = {
        "norm": per_layer(norm_w), "win": w_in.astype(BF16), "wout": w_out.astype(BF16),
        "mu": per_layer(rw_mu), "w0": per_layer(rw_w0), "a0": per_layer(rw_a0),
        "wcode": jnp.concatenate([jnp.concatenate([rw_w_up, zeros_code], axis=2),
                                  jnp.concatenate([zeros_code, rw_a_up], axis=2)], axis=1).astype(BF16),
        "kk": per_layer(rw_k_k), "ka": per_layer(rw_k_a), "rk": per_layer(rw_r_k),
        "lnw": per_layer(rw_ln_w), "lnb": per_layer(rw_ln_b), "hgnw": per_layer(hg_norm_w),
        "convw": lru_conv_w, "convb": per_layer(lru_conv_b),
        "wa": _block_diag(lru_wa).astype(BF16), "ba": per_layer(lru_ba),
        "wx": _block_diag(lru_wx).astype(BF16), "bx": per_layer(lru_bx), "lam": per_layer(lru_lambda),
    }
    vres = {"v0": per_layer(rw_v0), "vdn": rw_v_dn.astype(BF16), "vup": rw_v_up.astype(BF16)}
    v_first = None
    for l in range(depth):
        p = {k: (a, l) for k, a in stacked.items()}
        p["lbraw"] = (hg_lb_raw, None)
        if l > 0:
            p.update({k: (a, l - 1) for k, a in vres.items()})
        prep = _inproj(x2, p["norm"], p["win"], v_first, p, seq)
        if l == 0:
            v_first = prep[5]
        fw = final_norm_w.reshape(1, -1) if l == depth - 1 else None
        x2 = _mixer(prep[:5], x2, p["wout"], fw, l, p, batch, seq)
    return x2.reshape(batch, seq, d)
```

```python
import functools

import numpy as np
import jax
import jax.numpy as jnp
from jax import lax
from jax.experimental import pallas as pl
from jax.experimental.pallas import tpu as pltpu

F32 = jnp.float32
BF16 = jnp.bfloat16

NORM_EPS = 1e-6
RW_GN_EPS = 64e-5
LRU_C = 8.0
HEAD = 64
LANES = 128
SUBLANES = 8
CHUNK = 64
HG_SUB = 4
HG_LEVELS = (8, 16, 32, 64)
CONV_WIDTH = 4
LOG2_HEAD = HEAD.bit_length() - 1
LOG2_CHUNK = CHUNK.bit_length() - 1
LOG2_E = 1.4426950408889634
CHUNKS_PER_ITER = 2
V7X_VMEM_BYTES = 64 * 1024 * 1024


def _dot(a, b):
    return jnp.dot(a, b, preferred_element_type=F32)


def _dot_nt(a, b):
    return lax.dot_general(a, b, (((1,), (1,)), ((), ())), preferred_element_type=F32)


def _dot_tn(a, b):
    return lax.dot_general(a, b, (((0,), (0,)), ((), ())), preferred_element_type=F32)


def _mm(a, b):
    return _dot(a.astype(BF16), b.astype(BF16))


def _split2(x):
    hi = x.astype(BF16)
    lo = (x - hi.astype(F32)).astype(BF16)
    return hi, lo


def _sigmoid(x):
    return 1.0 / (1.0 + jnp.exp(-x))


def _silu(x):
    return x * _sigmoid(x)


def _softplus(x):
    return jnp.maximum(x, 0.0) + jnp.log1p(jnp.exp(-jnp.abs(x)))


def _iota(shape, dim):
    return lax.broadcasted_iota(jnp.int32, shape, dim)


def _head_sums(x, seg256):
    xb = x.astype(BF16)
    seg128 = seg256[0:LANES, 0:LANES]
    return jnp.concatenate([_dot(xb[:, 0:2 * LANES], seg256), _dot(xb[:, 2 * LANES:3 * LANES], seg128)], axis=1)


def _inproj_kernel(*refs, has_vres, rw_w, tiles_per_seq):
    it = iter(refs)
    x_ref, nw_ref, w_ref = (next(it) for _ in range(3))
    vfirst_in_ref = next(it) if has_vres else None
    mu_ref, w0_ref, a0_ref, wcode_ref = (next(it) for _ in range(4))
    if has_vres:
        v0_ref, vdn_ref, vup_ref = (next(it) for _ in range(3))
    kk_ref, ka_ref, rk_ref, tri2_ref, seg256_ref = (next(it) for _ in range(5))
    pack_ref, v_ref, bonus_ref, elast_ref, rest_ref = (next(it) for _ in range(5))
    vfirst_out_ref = None if has_vres else next(it)
    prev_ref = next(it)

    tm = x_ref.shape[0]
    sub = 2 * CHUNK
    n_sub = tm // sub
    rw_shift = 3 * rw_w + LANES
    rest_tiles = (w_ref.shape[1] - rw_shift) // LANES

    @pl.when(lax.rem(pl.program_id(0), tiles_per_seq) == 0)
    def _new_sequence():
        prev_ref[...] = jnp.zeros_like(prev_ref)

    x = x_ref[...]
    ms = jnp.mean(x * x, axis=-1, keepdims=True)
    hb = (x * lax.rsqrt(ms + NORM_EPS) * nw_ref[...]).astype(BF16)
    u_rw = _dot(hb, w_ref[:, 0:rw_shift])

    seg256 = seg256_ref[...]
    row = _iota((sub, 1), 0)
    lane_first_half = _iota((1, LANES), 1) < LANES // 2
    prev = prev_ref[...]
    for s in range(n_sub):
        lo = rw_shift + (rest_tiles * s // n_sub) * LANES
        hi = rw_shift + (rest_tiles * (s + 1) // n_sub) * LANES
        rest_ref[:, lo - rw_shift:hi - rw_shift] = _dot(hb, w_ref[:, lo:hi])

        rs = slice(s * sub, (s + 1) * sub)
        ru = u_rw[rs]
        sh = jnp.where(row == 0, prev, pltpu.roll(ru, 1, axis=0))
        prev = ru[sub - 1:sub, :]
        ul = ru + mu_ref[...] * (sh - ru)
        r = ul[:, 0:rw_w]
        k = ul[:, rw_w:2 * rw_w]
        v = ul[:, 2 * rw_w:3 * rw_w]
        codes = ul[:, 3 * rw_w:rw_shift]
        both = _mm(jnp.where(lane_first_half, jnp.tanh(codes), codes), wcode_ref[...])
        w_log = -_softplus(-(w0_ref[...] + both[:, 0:rw_w])) - 0.5
        nlw2 = jnp.exp(w_log) * LOG2_E
        a = _sigmoid(a0_ref[...] + both[:, rw_w:2 * rw_w])
        if has_vres:
            gate = _sigmoid(v0_ref[...] + _mm(_mm(v, vdn_ref[...]), vup_ref[...]))
            v = v + (vfirst_in_ref[rs, :] - v) * gate
        else:
            vfirst_out_ref[rs, :] = v
        kk = k * kk_ref[...]
        k2 = k * (1.0 + (a - 1.0) * ka_ref[...])
        sums = _head_sums(jnp.concatenate([kk * kk, r * k2 * rk_ref[...]], axis=0), seg256)
        kk = kk / jnp.maximum(jnp.sqrt(sums[0:sub]), 1e-12)
        bonus_ref[rs, :] = sums[sub:2 * sub] * v
        v_ref[rs, :] = v.astype(BF16)
        cs = _dot(tri2_ref[...], jnp.concatenate(_split2(nlw2), axis=0))
        e_pos = jnp.exp2(-cs)
        e_neg = jnp.exp2(cs)
        pack_ref[rs, 0:rw_w] = (-kk * jnp.exp2(nlw2 - cs)).astype(BF16)
        pack_ref[rs, rw_w:2 * rw_w] = (r * e_pos).astype(BF16)
        pack_ref[rs, 2 * rw_w:3 * rw_w] = (kk * a * e_neg).astype(BF16)
        pack_ref[rs, 3 * rw_w:4 * rw_w] = (k2 * e_neg).astype(BF16)
        for c in range(sub // CHUNK):
            n = s * (sub // CHUNK) + c
            elast_ref[n:n + 1, :] = e_pos[(c + 1) * CHUNK - 1:(c + 1) * CHUNK, :]
    prev_ref[...] = prev


def _row_block(n_rows, target):
    blk = min(target, n_rows)
    while n_rows % blk:
        blk //= 2
    return blk


def _vmem_limit(block_bytes):
    return int(min(V7X_VMEM_BYTES * 7 // 8, 2 * block_bytes + 16 * 1024 * 1024))


def _layer_spec(a, idx):
    nd = a.ndim
    return pl.BlockSpec((None,) + a.shape[1:], lambda *_: (idx,) + (0,) * (nd - 1))


def _inproj(x2, norm_w, w_bf16, v_first, p, seq):
    n, d = x2.shape
    n_in = w_bf16[0].shape[-1]
    rw_w = p["w0"][0].shape[-1]
    rw_shift = 3 * rw_w + LANES
    tm = _row_block(seq, 512)
    assert tm % (SUBLANES * CHUNK) == 0 and seq % tm == 0
    has_vres = v_first is not None

    def tok_spec(width):
        return pl.BlockSpec((tm, width), lambda i: (i, 0))

    def full_spec(a):
        nd = a.ndim
        return pl.BlockSpec(a.shape, lambda i: (0,) * nd)

    tri2, _, seg256 = _mixer_constants(2)
    args, in_specs = [x2], [tok_spec(d)]
    for a, idx in (norm_w, w_bf16):
        args.append(a)
        in_specs.append(_layer_spec(a, idx))
    if has_vres:
        args.append(v_first)
        in_specs.append(tok_spec(rw_w))
    names = ["mu", "w0", "a0", "wcode"] + (["v0", "vdn", "vup"] if has_vres else []) + ["kk", "ka", "rk"]
    for a, idx in [p[k] for k in names]:
        args.append(a)
        in_specs.append(_layer_spec(a, idx))
    for a in (tri2, seg256):
        args.append(a)
        in_specs.append(full_spec(a))
    out_shape = [jax.ShapeDtypeStruct((n, 4 * rw_w), BF16), jax.ShapeDtypeStruct((n, rw_w), BF16),
                 jax.ShapeDtypeStruct((n, rw_w), F32), jax.ShapeDtypeStruct((n // CHUNK, rw_w), F32),
                 jax.ShapeDtypeStruct((n, n_in - rw_shift), F32)]
    out_specs = [tok_spec(4 * rw_w), tok_spec(rw_w), tok_spec(rw_w),
                 pl.BlockSpec((tm // CHUNK, rw_w), lambda i: (i, 0)), tok_spec(n_in - rw_shift)]
    if not has_vres:
        out_shape.append(jax.ShapeDtypeStruct((n, rw_w), F32))
        out_specs.append(tok_spec(rw_w))
    block_bytes = tm * d * 4 + d * n_in * 2 + tm * (n_in + 4 * rw_w) * 4
    body = functools.partial(_inproj_kernel, has_vres=has_vres, rw_w=rw_w, tiles_per_seq=seq // tm)
    return pl.pallas_call(
        body,
        out_shape=out_shape,
        grid=(n // tm,),
        in_specs=in_specs,
        out_specs=out_specs,
        scratch_shapes=[pltpu.VMEM((1, rw_shift), F32)],
        compiler_params=pltpu.CompilerParams(
            dimension_semantics=("arbitrary",), vmem_limit_bytes=_vmem_limit(block_bytes)),
        name="inproj_vres" if has_vres else "inproj",
    )(*args)


def _stack_heads(x, lane_head0):
    return jnp.concatenate([jnp.where(lane_head0, x, 0.0), jnp.where(lane_head0, 0.0, x)], axis=0)


def _mix_kernel(*refs, layer, final, n_chunks, nb, rw_w, hg_w, lru_w):
    it = iter(refs)
    pack_ref, v_ref, bonus_ref, elast_ref, u_ref, x_ref, wout_ref = (next(it) for _ in range(7))
    fw_ref = next(it) if final else None
    lnw_ref, lnb_ref, lbraw_ref, hgnw_ref = (next(it) for _ in range(4))
    convw_ref, convb_ref, wa_ref, ba_ref, wx_ref, bx_ref, lam_ref = (next(it) for _ in range(7))
    hgm_ref, seg256_ref = (next(it) for _ in range(2))
    xout_ref = next(it)
    out_ref, rwstate_ref, hgstate_ref, lrux_ref, lruh_ref = (next(it) for _ in range(5))

    n_pairs = rw_w // LANES
    rows_all = nb * CHUNK
    o_rwz = 0
    o_hgq = o_rwz + rw_w
    o_hgf = o_hgq + hg_w
    o_hgi = o_hgf + hg_w
    o_hgz = o_hgi + hg_w
    o_lrx = o_hgz + hg_w
    o_lrz = o_lrx + lru_w

    @pl.when(pl.program_id(0) == 0)
    def _reset():
        rwstate_ref[...] = jnp.zeros_like(rwstate_ref)
        hgstate_ref[...] = jnp.zeros_like(hgstate_ref)
        lrux_ref[...] = jnp.zeros_like(lrux_ref)
        lruh_ref[...] = jnp.zeros_like(lruh_ref)

    row = _iota((rows_all, 1), 0)
    tpos = row & (CHUNK - 1)
    lane_head0 = _iota((1, LANES), 1) < HEAD
    ti = _iota((CHUNK, LANES), 0)
    tj = _iota((CHUNK, LANES), 1) & (CHUNK - 1)
    strict_lower = tj < ti
    incl_lower = tj <= ti
    eye = jnp.where(tj == ti, 1.0, 0.0)
    same_group = {size: (ti >> (size.bit_length() - 1)) == (tj >> (size.bit_length() - 1)) for size in HG_LEVELS}
    bd_state = (_iota((LANES, LANES), 0) >> LOG2_HEAD) == (_iota((LANES, LANES), 1) >> LOG2_HEAD)
    pairs = [(b, p) for p in range(n_pairs) for b in range(nb)]

    seg256 = seg256_ref[...]

    def segsum(x):
        return _head_sums(x, seg256)

    def stacked(m):
        return _stack_heads(m, lane_head0).astype(BF16)

    def per_batch(ref):
        val = ref[0]
        for b in range(1, nb):
            val = jnp.where(row >= b * CHUNK, ref[b], val)
        return val

    def tokens(ref, rows, lo, hi):
        return jnp.concatenate([ref[b, rows, lo:hi] for b in range(nb)], axis=0)

    def store(rows, lo, val):
        for b in range(nb):
            out_ref[b, rows, lo:lo + val.shape[1]] = val[b * CHUNK:(b + 1) * CHUNK].astype(out_ref.dtype)

    def blk(x, b, p):
        return x[b * CHUNK:(b + 1) * CHUNK, p * LANES:(p + 1) * LANES]

    lbraw = lbraw_ref[...]
    lbe = jnp.exp(lbraw - jnp.max(lbraw, axis=0, keepdims=True))
    lbw = lbe / jnp.sum(lbe, axis=0, keepdims=True)
    lb = lbw[0:1, :]
    for j in range(1, layer + 1):
        lb = lb + lbw[j:j + 1, :]
    lb = lb - lbw[0:1, :]

    def rwkv_stages(chunks):
        insts = [(ci, b, p) for ci in range(len(chunks)) for (b, p) in pairs]
        lhs, rhs, vs, a_ak, lrk, avs, pw, tinv = {}, {}, {}, {}, {}, {}, {}, {}
        for ci, (_, rows) in enumerate(chunks):
            a_t = tokens(pack_ref, rows, 0, rw_w)
            r_t = tokens(pack_ref, rows, rw_w, 2 * rw_w)
            b_t = tokens(pack_ref, rows, 2 * rw_w, 3 * rw_w)
            k_t = tokens(pack_ref, rows, 3 * rw_w, 4 * rw_w)
            v = tokens(v_ref, rows, 0, rw_w)
            for b, p in pairs:
                i = (ci, b, p)
                lhs[i] = jnp.concatenate([blk(a_t, b, p), blk(r_t, b, p)], axis=0)
                rhs[i] = jnp.concatenate([_stack_heads(blk(b_t, b, p), lane_head0),
                                          _stack_heads(blk(k_t, b, p), lane_head0)], axis=0)
                vs[i] = _stack_heads(blk(v, b, p), lane_head0)
        yield
        for i in insts:
            sc = _dot_nt(lhs[i], rhs[i])
            pw[i] = jnp.where(strict_lower, sc[0:CHUNK, 0:LANES], 0.0)
            a_ak[i] = jnp.where(strict_lower, sc[0:CHUNK, LANES:2 * LANES], 0.0).astype(BF16)
            lrk[i] = jnp.concatenate([jnp.where(incl_lower, sc[CHUNK:2 * CHUNK, 0:LANES], 0.0),
                                      jnp.where(incl_lower, sc[CHUNK:2 * CHUNK, LANES:2 * LANES], 0.0)],
                                     axis=1).astype(BF16)
        yield
        for i in insts:
            avs[i] = _dot(a_ak[i], vs[i])
            tinv[i] = eye + pw[i]
            pw[i] = _dot(pw[i].astype(BF16), stacked(pw[i]))
        yield
        for s in range(1, 5):
            for i in insts:
                both = _dot(jnp.concatenate([pw[i], tinv[i]], axis=0).astype(BF16), stacked(pw[i]))
                pw[i] = both[0:CHUNK]
                tinv[i] = tinv[i] + both[CHUNK:2 * CHUNK]
            yield
        for i in insts:
            tinv[i] = (tinv[i] + _dot(tinv[i].astype(BF16), stacked(pw[i]))).astype(BF16)
        yield
        ys = []
        for ci, (c, rows) in enumerate(chunks):
            st, am, uv, yb = {}, {}, {}, {}
            for n, (b, p) in enumerate(pairs):
                st[b, p] = rwstate_ref[n]
                am[b, p] = _dot_nt(lhs[ci, b, p], st[b, p].astype(BF16))
            yield
            for b, p in pairs:
                i = (ci, b, p)
                x = am[b, p][0:CHUNK] + avs[i]
                uv[b, p] = jnp.concatenate([stacked(_dot(tinv[i], stacked(x))), vs[i]], axis=0)
            yield
            for b, p in pairs:
                yb[b, p] = am[b, p][CHUNK:2 * CHUNK] + _dot(lrk[ci, b, p], uv[b, p])
            yield
            for n, (b, p) in enumerate(pairs):
                rwstate_ref[n] = ((st[b, p] + _dot_tn(uv[b, p], rhs[ci, b, p]))
                                  * elast_ref[b, c][:, p * LANES:(p + 1) * LANES])
            yield
            ys.append(jnp.concatenate(
                [jnp.concatenate([yb[b, p] for p in range(n_pairs)], axis=1) for b in range(nb)], axis=0))
        y = jnp.concatenate(ys, axis=0)
        mean = segsum(y) * (1.0 / HEAD)
        yc = y - mean
        yield
        var = segsum(yc * yc) * (1.0 / HEAD)
        yn = yc * lax.rsqrt(var + RW_GN_EPS) * lnw_ref[...] + lnb_ref[...]
        for ci, (_, rows) in enumerate(chunks):
            store(rows, 0, (yn[ci * rows_all:(ci + 1) * rows_all] + tokens(bonus_ref, rows, 0, rw_w))
                  * _silu(tokens(u_ref, rows, o_rwz, o_rwz + rw_w)))
            yield

    def hgrn_stages(c, rows):
        q = tokens(u_ref, rows, o_hgq, o_hgq + hg_w)
        f = lb + (1.0 - lb) * _sigmoid(tokens(u_ref, rows, o_hgf, o_hgf + hg_w))
        g = jnp.log2(f)
        kx = 1.0 - f
        iv = tokens(u_ref, rows, o_hgi, o_hgi + hg_w)
        bb = _dot(hgm_ref[...], jnp.concatenate(_split2(g), axis=0))
        part = lambda n: bb[n * rows_all:(n + 1) * rows_all]
        bsub = part(0)
        n_lv = len(HG_LEVELS)
        yield
        bchunk = part(n_lv + 1)
        qc = (q * jnp.exp2(bchunk)).astype(BF16)
        kc = (kx * jnp.exp2(part(n_lv + 2))).astype(BF16)
        ivb = iv.astype(BF16)
        ocs = {}
        for n, (b, p) in enumerate(pairs):
            hst = hgstate_ref[n]
            ocs[b, p] = _dot_nt(blk(qc, b, p), hst.astype(BF16))
            upd = jnp.where(bd_state, _dot_tn(blk(ivb, b, p), blk(kc, b, p)), 0.0)
            last = (b + 1) * CHUNK - 1
            hgstate_ref[n] = hst * jnp.exp2(bchunk[last:last + 1, p * LANES:(p + 1) * LANES]) + upd
        yield
        scs = {i: None for i in pairs}
        for lv, size in enumerate(HG_LEVELS):
            pst = jnp.exp2(part(1 + lv))
            upper = (row & (size - 1)) >= size // 2
            qs = jnp.where(upper, q * pst, 0.0)
            ks = jnp.where(upper, 0.0, kx * pst)
            for i in pairs:
                sc = jnp.where(same_group[size], _dot_nt(blk(qs, *i).astype(BF16), stacked(blk(ks, *i))), 0.0)
                scs[i] = sc if scs[i] is None else scs[i] + sc
            yield
        ostr = {i: _mm(scs[i], stacked(blk(iv, *i))) + ocs[i] for i in pairs}
        yield
        sub = (rows_all // SUBLANES, SUBLANES, hg_w)
        pos = _iota((1, SUBLANES, 1), 1) & (HG_SUB - 1)
        q3, b3, k3, v3 = (z.reshape(sub) for z in (q, bsub, kx, iv))
        wgts = [q * kx]
        for d in range(1, HG_SUB):
            wgt = jnp.where(pos >= d, q3 * jnp.exp2(b3 - pltpu.roll(b3, d, axis=1)) * pltpu.roll(k3, d, axis=1), 0.0)
            wgts.append(wgt.reshape(rows_all, hg_w))
        yield
        sums = segsum(jnp.concatenate(wgts, axis=0))
        yield
        acc = sums[0:rows_all] * iv
        for d in range(1, HG_SUB):
            acc = acc + sums[d * rows_all:(d + 1) * rows_all] * pltpu.roll(v3, d, axis=1).reshape(rows_all, hg_w)
        o = acc + jnp.concatenate(
            [jnp.concatenate([ostr[(b, p)] for p in range(n_pairs)], axis=1) for b in range(nb)], axis=0)
        ms = segsum(o * o) * (1.0 / HEAD)
        store(rows, rw_w, o * lax.rsqrt(ms + NORM_EPS) * hgnw_ref[...] * _silu(tokens(u_ref, rows, o_hgz, o_hgz + hg_w)))

    def lru_stages(c, rows):
        xb = tokens(u_ref, rows, o_lrx, o_lrx + lru_w)
        row8 = _iota((SUBLANES, 1), 0)
        yv = convb_ref[...] + convw_ref[CONV_WIDTH - 1:CONV_WIDTH, :] * xb
        for d in range(1, CONV_WIDTH):
            rolled = pltpu.roll(xb, d, axis=0)
            pieces = []
            for b in range(nb):
                tail = pltpu.roll(lrux_ref[b], d, axis=0)
                pieces.append(jnp.where(row8 < d, tail, rolled[b * CHUNK:b * CHUNK + SUBLANES]))
                pieces.append(rolled[b * CHUNK + SUBLANES:(b + 1) * CHUNK])
            yv = yv + convw_ref[CONV_WIDTH - 1 - d:CONV_WIDTH - d, :] * jnp.concatenate(pieces, axis=0)
        for b in range(nb):
            lrux_ref[b] = xb[(b + 1) * CHUNK - SUBLANES:(b + 1) * CHUNK]
        yield
        ybf = yv.astype(BF16)
        rg = _sigmoid(_dot(ybf, wa_ref[...]) + ba_ref[...])
        ig = _sigmoid(_dot(ybf, wx_ref[...]) + bx_ref[...])
        log_a = -LRU_C * rg * _softplus(-lam_ref[...])
        av = jnp.exp(log_a)
        th = jnp.tanh(log_a)
        gu = jnp.sqrt(-2.0 * th / (1.0 - th)) * (ig * yv)
        yield
        step = 1
        while step < CHUNK:
            keep = tpos >= step
            a_sh = jnp.where(keep, pltpu.roll(av, step, axis=0), 1.0)
            u_sh = jnp.where(keep, pltpu.roll(gu, step, axis=0), 0.0)
            gu = av * u_sh + gu
            av = av * a_sh
            step *= 2
            yield
        hv = gu + av * per_batch(lruh_ref)
        for b in range(nb):
            lruh_ref[b] = hv[(b + 1) * CHUNK - 1:(b + 1) * CHUNK]
        store(rows, rw_w + hg_w, hv * _silu(tokens(u_ref, rows, o_lrz, o_lrz + lru_w)))

    def project_rows(r0, n_rows):
        mixed = jnp.concatenate([out_ref[b, r0:r0 + n_rows, :] for b in range(nb)], axis=0)
        proj = _dot(mixed, wout_ref[...])
        yield
        for b in range(nb):
            xn = x_ref[b, r0:r0 + n_rows, :] + proj[b * n_rows:(b + 1) * n_rows]
            if final:
                ms = jnp.mean(xn * xn, axis=-1, keepdims=True)
                xn = xn * lax.rsqrt(ms + NORM_EPS) * fw_ref[...]
            xout_ref[b, r0:r0 + n_rows, :] = xn
        yield

    assert n_chunks % CHUNKS_PER_ITER == 0
    group_rows = CHUNKS_PER_ITER * CHUNK
    for gi in range(n_chunks // CHUNKS_PER_ITER):
        chunks = [(gi * CHUNKS_PER_ITER + j, pl.ds((gi * CHUNKS_PER_ITER + j) * CHUNK, CHUNK))
                  for j in range(CHUNKS_PER_ITER)]
        pending = [rwkv_stages(chunks)] + [hgrn_stages(*ch) for ch in chunks] + [lru_stages(*ch) for ch in chunks]
        if gi > 0:
            pending.append(project_rows((gi - 1) * group_rows, group_rows))
        while pending:
            for gen in list(pending):
                if next(gen, True):
                    pending.remove(gen)
    for _ in project_rows((n_chunks // CHUNKS_PER_ITER - 1) * group_rows, group_rows):
        pass


def _mixer_constants(nb):
    rows_all = nb * CHUNK
    t = np.arange(rows_all)
    same_chunk = (t[None, :] // CHUNK) == (t[:, None] // CHUNK)
    lower = t[None, :] <= t[:, None]
    tri2 = np.tile(lower & same_chunk, (1, 2))
    mats = [lower & ((t[None, :] // HG_SUB) == (t[:, None] // HG_SUB))]
    for size in HG_LEVELS:
        half = size // 2
        same = (t[None, :] // size) == (t[:, None] // size)
        upper_row = (t[:, None] % size) >= half
        upper_col = (t[None, :] % size) >= half
        mats.append(same & np.where(upper_row, upper_col & lower, (~upper_col) & (~lower)))
    mats.append(lower & same_chunk)
    mats.append((~lower) & same_chunk)
    hgm = np.tile(np.concatenate(mats, axis=0), (1, 2))
    h = np.arange(2 * LANES) // HEAD
    seg256 = h[None, :] == h[:, None]
    as_bf16 = lambda m: jnp.asarray(m.astype(np.float32), dtype=BF16)
    return as_bf16(tri2), as_bf16(hgm), as_bf16(seg256)


def _block_diag(w):
    nl, g, n, _ = w.shape
    out = jnp.zeros((nl, g * n, g * n), w.dtype)
    for i in range(g):
        out = out.at[:, i * n:(i + 1) * n, i * n:(i + 1) * n].set(w[:, i])
    return out


def _mixer(prep, x2, w_out_bf16, final_w, layer, p, batch, seq):
    pack, vb, bonus, elast, rest = prep
    d = x2.shape[1]
    rw_w = vb.shape[1]
    hg_w = p["hgnw"][0].shape[-1]
    lru_w = p["lam"][0].shape[-1]
    d_mix = rw_w + hg_w + lru_w
    tb = _row_block(seq, 256)
    assert tb % (CHUNKS_PER_ITER * CHUNK) == 0 and rw_w == 3 * LANES and hg_w == rw_w
    assert w_out_bf16[0].shape[-2] == d_mix

    def tok_spec(width):
        return pl.BlockSpec((batch, tb, width), lambda t: (0, t, 0))

    def full_spec(a):
        nd = a.ndim
        return pl.BlockSpec(a.shape, lambda t: (0,) * nd)

    per_token = lambda a: a.reshape(batch, seq, a.shape[1])
    args = [per_token(pack), per_token(vb), per_token(bonus), elast.reshape(batch, seq // CHUNK, 1, rw_w),
            per_token(rest), per_token(x2), w_out_bf16[0]]
    in_specs = [tok_spec(4 * rw_w), tok_spec(rw_w), tok_spec(rw_w),
                pl.BlockSpec((batch, tb // CHUNK, 1, rw_w), lambda t: (0, t, 0, 0)), tok_spec(rest.shape[1]),
                tok_spec(d), _layer_spec(*w_out_bf16)]
    if final_w is not None:
        args.append(final_w)
        in_specs.append(full_spec(final_w))
    names = ["lnw", "lnb", "lbraw", "hgnw", "convw", "convb", "wa", "ba", "wx", "bx", "lam"]
    for k in names:
        a, idx = p[k]
        args.append(a)
        in_specs.append(full_spec(a) if idx is None else _layer_spec(a, idx))
    _, hgm, seg256 = _mixer_constants(batch)
    for a in (hgm, seg256):
        args.append(a)
        in_specs.append(full_spec(a))

    n_pairs = rw_w // LANES
    scratch = [pltpu.VMEM((batch, tb, d_mix), BF16),
               pltpu.VMEM((batch * n_pairs, LANES, LANES), F32),
               pltpu.VMEM((batch * n_pairs, LANES, LANES), F32),
               pltpu.VMEM((batch, SUBLANES, lru_w), F32),
               pltpu.VMEM((batch, 1, lru_w), F32)]
    block_bytes = (batch * tb * (rest.shape[1] * 4 + 4 * rw_w * 2 + rw_w * 2 + rw_w * 4 + d_mix * 2 + 2 * d * 4)
                   + d_mix * d * 2)
    body = functools.partial(_mix_kernel, layer=layer, final=final_w is not None, n_chunks=tb // CHUNK, nb=batch,
                             rw_w=rw_w, hg_w=hg_w, lru_w=lru_w)
    x_new = pl.pallas_call(
        body,
        out_shape=jax.ShapeDtypeStruct((batch, seq, d), F32),
        grid=(seq // tb,),
        in_specs=in_specs,
        out_specs=tok_spec(d),
        scratch_shapes=scratch,
        compiler_params=pltpu.CompilerParams(
            dimension_semantics=("arbitrary",), vmem_limit_bytes=_vmem_limit(block_bytes)),
        name=f"mixer_l{layer}",
    )(*args)
    return x_new.reshape(batch * seq, d)


def kernel(x, norm_w, w_in, rw_mu, rw_w0, rw_w_up, rw_a0, rw_a_up, rw_v0, rw_v_dn, rw_v_up, rw_k_k, rw_k_a, rw_r_k, rw_ln_w, rw_ln_b, hg_lb_raw, hg_norm_w, lru_conv_w, lru_conv_b, lru_wa, lru_ba, lru_wx, lru_bx, lru_lambda, w_out, final_norm_w):
    batch, seq, d = x.shape
    depth = w_in.shape[0]
    rw_w = rw_w0.shape[1]
    lora = rw_w_up.shape[1]
    assert 2 * lora == LANES
    x2 = x.reshape(batch * seq, d)
    zeros_code = jnp.zeros((depth, lora, rw_w), F32)
    per_layer = lambda a: a.reshape(a.shape[0], 1, -1)
    stacked = {
        "norm": per_layer(norm_w), "win": w_in.astype(BF16), "wout": w_out.astype(BF16),
        "mu": per_layer(rw_mu), "w0": per_layer(rw_w0), "a0": per_layer(rw_a0),
        "wcode": jnp.concatenate([jnp.concatenate([rw_w_up, zeros_code], axis=2),
                                  jnp.concatenate([zeros_code, rw_a_up], axis=2)], axis=1).astype(BF16),
        "kk": per_layer(rw_k_k), "ka": per_layer(rw_k_a), "rk": per_layer(rw_r_k),
        "lnw": per_layer(rw_ln_w), "lnb": per_layer(rw_ln_b), "hgnw": per_layer(hg_norm_w),
        "convw": lru_conv_w, "convb": per_layer(lru_conv_b),
        "wa": _block_diag(lru_wa).astype(BF16), "ba": per_layer(lru_ba),
        "wx": _block_diag(lru_wx).astype(BF16), "bx": per_layer(lru_bx), "lam": per_layer(lru_lambda),
    }
    vres = {"v0": per_layer(rw_v0), "vdn": rw_v_dn.astype(BF16), "vup": rw_v_up.astype(BF16)}
    v_first = None
    for l in range(depth):
        p = {k: (a, l) for k, a in stacked.items()}
        p["lbraw"] = (hg_lb_raw, None)
        if l > 0:
            p.update({k: (a, l - 1) for k, a in vres.items()})
        prep = _inproj(x2, p["norm"], p["win"], v_first, p, seq)
        if l == 0:
            v_first = prep[5]
        fw = final_norm_w.reshape(1, -1) if l == depth - 1 else None
        x2 = _mixer(prep[:5], x2, p["wout"], fw, l, p, batch, seq)
    return x2.reshape(batch, seq, d)
```

```python
import functools

import numpy as np
import jax
import jax.numpy as jnp
from jax import lax
from jax.experimental import pallas as pl
from jax.experimental.pallas import tpu as pltpu

F32 = jnp.float32
BF16 = jnp.bfloat16

NORM_EPS = 1e-6
RW_GN_EPS = 64e-5
LRU_C = 8.0
HEAD = 64
LANES = 128
SUBLANES = 8
CHUNK = 64
HG_SUB = 4
HG_LEVELS = (8, 16, 32, 64)
CONV_WIDTH = 4
LOG2_HEAD = HEAD.bit_length() - 1
LOG2_E = 1.4426950408889634
CHUNKS_PER_ITER = 2
V7X_VMEM_BYTES = 64 * 1024 * 1024


def _dot(a, b):
    return jnp.dot(a, b, preferred_element_type=F32)


def _dot_nt(a, b):
    return lax.dot_general(a, b, (((1,), (1,)), ((), ())), preferred_element_type=F32)


def _dot_tn(a, b):
    return lax.dot_general(a, b, (((0,), (0,)), ((), ())), preferred_element_type=F32)


def _mm(a, b):
    return _dot(a.astype(BF16), b.astype(BF16))


def _split2(x):
    hi = x.astype(BF16)
    lo = (x - hi.astype(F32)).astype(BF16)
    return hi, lo


def _sigmoid(x):
    return 1.0 / (1.0 + jnp.exp(-x))


def _silu(x):
    return x * _sigmoid(x)


def _softplus(x):
    return jnp.maximum(x, 0.0) + jnp.log1p(jnp.exp(-jnp.abs(x)))


def _iota(shape, dim):
    return lax.broadcasted_iota(jnp.int32, shape, dim)


def _head_sums(x, seg256):
    xb = x.astype(BF16)
    seg128 = seg256[0:LANES, 0:LANES]
    return jnp.concatenate([_dot(xb[:, 0:2 * LANES], seg256), _dot(xb[:, 2 * LANES:3 * LANES], seg128)], axis=1)


def _inproj_kernel(*refs, has_vres, rw_w, tiles_per_seq):
    it = iter(refs)
    x_ref, nw_ref, w_ref = (next(it) for _ in range(3))
    vfirst_in_ref = next(it) if has_vres else None
    mu_ref, w0_ref, a0_ref, wcode_ref = (next(it) for _ in range(4))
    if has_vres:
        v0_ref, vdn_ref, vup_ref = (next(it) for _ in range(3))
    kk_ref, ka_ref, rk_ref, tri2_ref, seg256_ref = (next(it) for _ in range(5))
    pack_ref, v_ref, bonus_ref, elast_ref, rest_ref = (next(it) for _ in range(5))
    vfirst_out_ref = None if has_vres else next(it)
    prev_ref = next(it)

    tm = x_ref.shape[0]
    sub = 2 * CHUNK
    n_sub = tm // sub
    rw_shift = 3 * rw_w + LANES
    rest_tiles = (w_ref.shape[1] - rw_shift) // LANES

    @pl.when(lax.rem(pl.program_id(0), tiles_per_seq) == 0)
    def _new_sequence():
        prev_ref[...] = jnp.zeros_like(prev_ref)

    x = x_ref[...]
    ms = jnp.mean(x * x, axis=-1, keepdims=True)
    hb = (x * lax.rsqrt(ms + NORM_EPS) * nw_ref[...]).astype(BF16)
    u_rw = _dot(hb, w_ref[:, 0:rw_shift])

    seg256 = seg256_ref[...]
    row = _iota((sub, 1), 0)
    lane_first_half = _iota((1, LANES), 1) < LANES // 2
    prev = prev_ref[...]
    for s in range(n_sub):
        lo = rw_shift + (rest_tiles * s // n_sub) * LANES
        hi = rw_shift + (rest_tiles * (s + 1) // n_sub) * LANES
        rest_ref[:, lo - rw_shift:hi - rw_shift] = _dot(hb, w_ref[:, lo:hi])

        rs = slice(s * sub, (s + 1) * sub)
        ru = u_rw[rs]
        sh = jnp.where(row == 0, prev, pltpu.roll(ru, 1, axis=0))
        prev = ru[sub - 1:sub, :]
        ul = ru + mu_ref[...] * (sh - ru)
        r = ul[:, 0:rw_w]
        k = ul[:, rw_w:2 * rw_w]
        v = ul[:, 2 * rw_w:3 * rw_w]
        codes = ul[:, 3 * rw_w:rw_shift]
        both = _mm(jnp.where(lane_first_half, jnp.tanh(codes), codes), wcode_ref[...])
        w_log = -_softplus(-(w0_ref[...] + both[:, 0:rw_w])) - 0.5
        nlw2 = jnp.exp(w_log) * LOG2_E
        a = _sigmoid(a0_ref[...] + both[:, rw_w:2 * rw_w])
        if has_vres:
            gate = _sigmoid(v0_ref[...] + _mm(_mm(v, vdn_ref[...]), vup_ref[...]))
            v = v + (vfirst_in_ref[rs, :] - v) * gate
        else:
            vfirst_out_ref[rs, :] = v
        kk = k * kk_ref[...]
        k2 = k * (1.0 + (a - 1.0) * ka_ref[...])
        sums = _head_sums(jnp.concatenate([kk * kk, r * k2 * rk_ref[...]], axis=0), seg256)
        kk = kk / jnp.maximum(jnp.sqrt(sums[0:sub]), 1e-12)
        bonus_ref[rs, :] = sums[sub:2 * sub] * v
        v_ref[rs, :] = v.astype(BF16)
        cs = _dot(tri2_ref[...], jnp.concatenate(_split2(nlw2), axis=0))
        e_pos = jnp.exp2(-cs)
        e_neg = jnp.exp2(cs)
        pack_ref[rs, 0:rw_w] = (-kk * jnp.exp2(nlw2 - cs)).astype(BF16)
        pack_ref[rs, rw_w:2 * rw_w] = (r * e_pos).astype(BF16)
        pack_ref[rs, 2 * rw_w:3 * rw_w] = (kk * a * e_neg).astype(BF16)
        pack_ref[rs, 3 * rw_w:4 * rw_w] = (k2 * e_neg).astype(BF16)
        for c in range(sub // CHUNK):
            n = s * (sub // CHUNK) + c
            elast_ref[n:n + 1, :] = e_pos[(c + 1) * CHUNK - 1:(c + 1) * CHUNK, :]
    prev_ref[...] = prev


def _row_block(n_rows, target):
    blk = min(target, n_rows)
    while n_rows % blk:
        blk //= 2
    return blk


def _vmem_limit(block_bytes):
    return int(min(V7X_VMEM_BYTES * 7 // 8, 2 * block_bytes + 16 * 1024 * 1024))


def _layer_spec(a, idx):
    nd = a.ndim
    return pl.BlockSpec((None,) + a.shape[1:], lambda *_: (idx,) + (0,) * (nd - 1))


def _inproj(x2, norm_w, w_bf16, v_first, p, seq):
    n, d = x2.shape
    n_in = w_bf16[0].shape[-1]
    rw_w = p["w0"][0].shape[-1]
    rw_shift = 3 * rw_w + LANES
    tm = _row_block(seq, 512)
    assert tm % (SUBLANES * CHUNK) == 0 and seq % tm == 0
    has_vres = v_first is not None

    def tok_spec(width):
        return pl.BlockSpec((tm, width), lambda i: (i, 0))

    def full_spec(a):
        nd = a.ndim
        return pl.BlockSpec(a.shape, lambda i: (0,) * nd)

    tri2, _, seg256 = _mixer_constants(2)
    args, in_specs = [x2], [tok_spec(d)]
    for a, idx in (norm_w, w_bf16):
        args.append(a)
        in_specs.append(_layer_spec(a, idx))
    if has_vres:
        args.append(v_first)
        in_specs.append(tok_spec(rw_w))
    names = ["mu", "w0", "a0", "wcode"] + (["v0", "vdn", "vup"] if has_vres else []) + ["kk", "ka", "rk"]
    for a, idx in [p[k] for k in names]:
        args.append(a)
        in_specs.append(_layer_spec(a, idx))
    for a in (tri2, seg256):
        args.append(a)
        in_specs.append(full_spec(a))
    out_shape = [jax.ShapeDtypeStruct((n, 4 * rw_w), BF16), jax.ShapeDtypeStruct((n, rw_w), BF16),
                 jax.ShapeDtypeStruct((n, rw_w), F32), jax.ShapeDtypeStruct((n // CHUNK, rw_w), F32),
                 jax.ShapeDtypeStruct((n, n_in - rw_shift), F32)]
    out_specs = [tok_spec(4 * rw_w), tok_spec(rw_w), tok_spec(rw_w),
                 pl.BlockSpec((tm // CHUNK, rw_w), lambda i: (i, 0)), tok_spec(n_in - rw_shift)]
    if not has_vres:
        out_shape.append(jax.ShapeDtypeStruct((n, rw_w), F32))
        out_specs.append(tok_spec(rw_w))
    block_bytes = tm * d * 4 + d * n_in * 2 + tm * (n_in + 4 * rw_w) * 4
    body = functools.partial(_inproj_kernel, has_vres=has_vres, rw_w=rw_w, tiles_per_seq=seq // tm)
    return pl.pallas_call(
        body,
        out_shape=out_shape,
        grid=(n // tm,),
        in_specs=in_specs,
        out_specs=out_specs,
        scratch_shapes=[pltpu.VMEM((1, rw_shift), F32)],
        compiler_params=pltpu.CompilerParams(
            dimension_semantics=("arbitrary",), vmem_limit_bytes=_vmem_limit(block_bytes)),
        name="inproj_vres" if has_vres else "inproj",
    )(*args)


def _stack_heads(x, lane_head0):
    return jnp.concatenate([jnp.where(lane_head0, x, 0.0), jnp.where(lane_head0, 0.0, x)], axis=0)


def _mix_kernel(*refs, layer, final, n_chunks, nb, rw_w, hg_w, lru_w):
    it = iter(refs)
    pack_ref, v_ref, bonus_ref, elast_ref, u_ref, x_ref, wout_ref = (next(it) for _ in range(7))
    fw_ref = next(it) if final else None
    lnw_ref, lnb_ref, lbraw_ref, hgnw_ref = (next(it) for _ in range(4))
    convw_ref, convb_ref, wa_ref, ba_ref, wx_ref, bx_ref, lam_ref = (next(it) for _ in range(7))
    hgm_ref, seg256_ref = (next(it) for _ in range(2))
    xout_ref = next(it)
    out_ref, rwstate_ref, hgstate_ref, lrux_ref, lruh_ref = (next(it) for _ in range(5))

    n_pairs = rw_w // LANES
    rows_all = nb * CHUNK
    o_rwz = 0
    o_hgq = o_rwz + rw_w
    o_hgf = o_hgq + hg_w
    o_hgi = o_hgf + hg_w
    o_hgz = o_hgi + hg_w
    o_lrx = o_hgz + hg_w
    o_lrz = o_lrx + lru_w

    @pl.when(pl.program_id(0) == 0)
    def _reset():
        rwstate_ref[...] = jnp.zeros_like(rwstate_ref)
        hgstate_ref[...] = jnp.zeros_like(hgstate_ref)
        lrux_ref[...] = jnp.zeros_like(lrux_ref)
        lruh_ref[...] = jnp.zeros_like(lruh_ref)

    row = _iota((rows_all, 1), 0)
    tpos = row & (CHUNK - 1)
    lane_head0 = _iota((1, LANES), 1) < HEAD
    ti = _iota((CHUNK, LANES), 0)
    tj = _iota((CHUNK, LANES), 1) & (CHUNK - 1)
    strict_lower = tj < ti
    incl_lower = tj <= ti
    eye = jnp.where(tj == ti, 1.0, 0.0)
    same_group = {size: (ti >> (size.bit_length() - 1)) == (tj >> (size.bit_length() - 1)) for size in HG_LEVELS}
    bd_state = (_iota((LANES, LANES), 0) >> LOG2_HEAD) == (_iota((LANES, LANES), 1) >> LOG2_HEAD)
    pairs = [(b, p) for p in range(n_pairs) for b in range(nb)]

    seg256 = seg256_ref[...]

    def segsum(x):
        return _head_sums(x, seg256)

    def stacked(m):
        return _stack_heads(m, lane_head0).astype(BF16)

    def per_batch(ref):
        val = ref[0]
        for b in range(1, nb):
            val = jnp.where(row >= b * CHUNK, ref[b], val)
        return val

    def tokens(ref, rows, lo, hi):
        return jnp.concatenate([ref[b, rows, lo:hi] for b in range(nb)], axis=0)

    def store(rows, lo, val):
        for b in range(nb):
            out_ref[b, rows, lo:lo + val.shape[1]] = val[b * CHUNK:(b + 1) * CHUNK].astype(out_ref.dtype)

    def blk(x, b, p):
        return x[b * CHUNK:(b + 1) * CHUNK, p * LANES:(p + 1) * LANES]

    lbraw = lbraw_ref[...]
    lbe = jnp.exp(lbraw - jnp.max(lbraw, axis=0, keepdims=True))
    lbw = lbe / jnp.sum(lbe, axis=0, keepdims=True)
    lb = lbw[0:1, :]
    for j in range(1, layer + 1):
        lb = lb + lbw[j:j + 1, :]
    lb = lb - lbw[0:1, :]

    def rwkv_stages(chunks):
        insts = [(ci, b, p) for ci in range(len(chunks)) for (b, p) in pairs]
        lhs, rhs, vs, a_ak, lrk, avs, pw, tinv = {}, {}, {}, {}, {}, {}, {}, {}
        for ci, (_, rows) in enumerate(chunks):
            a_t = tokens(pack_ref, rows, 0, rw_w)
            r_t = tokens(pack_ref, rows, rw_w, 2 * rw_w)
            b_t = tokens(pack_ref, rows, 2 * rw_w, 3 * rw_w)
            k_t = tokens(pack_ref, rows, 3 * rw_w, 4 * rw_w)
            v = tokens(v_ref, rows, 0, rw_w)
            for b, p in pairs:
                i = (ci, b, p)
                lhs[i] = jnp.concatenate([blk(a_t, b, p), blk(r_t, b, p)], axis=0)
                rhs[i] = jnp.concatenate([_stack_heads(blk(b_t, b, p), lane_head0),
                                          _stack_heads(blk(k_t, b, p), lane_head0)], axis=0)
                vs[i] = _stack_heads(blk(v, b, p), lane_head0)
        yield
        for i in insts:
            sc = _dot_nt(lhs[i], rhs[i])
            pw[i] = jnp.where(strict_lower, sc[0:CHUNK, 0:LANES], 0.0)
            a_ak[i] = jnp.where(strict_lower, sc[0:CHUNK, LANES:2 * LANES], 0.0).astype(BF16)
            lrk[i] = jnp.concatenate([jnp.where(incl_lower, sc[CHUNK:2 * CHUNK, 0:LANES], 0.0),
                                      jnp.where(incl_lower, sc[CHUNK:2 * CHUNK, LANES:2 * LANES], 0.0)],
                                     axis=1).astype(BF16)
        yield
        for i in insts:
            avs[i] = _dot(a_ak[i], vs[i])
            tinv[i] = eye + pw[i]
            pw[i] = _dot(pw[i].astype(BF16), stacked(pw[i]))
        yield
        for s in range(1, 5):
            for i in insts:
                both = _dot(jnp.concatenate([pw[i], tinv[i]], axis=0).astype(BF16), stacked(pw[i]))
                pw[i] = both[0:CHUNK]
                tinv[i] = tinv[i] + both[CHUNK:2 * CHUNK]
            yield
        for i in insts:
            tinv[i] = (tinv[i] + _dot(tinv[i].astype(BF16), stacked(pw[i]))).astype(BF16)
        yield
        ys = []
        for ci, (c, rows) in enumerate(chunks):
            st, am, uv, yb = {}, {}, {}, {}
            for n, (b, p) in enumerate(pairs):
                st[b, p] = rwstate_ref[n]
                am[b, p] = _dot_nt(lhs[ci, b, p], st[b, p].astype(BF16))
            yield
            for b, p in pairs:
                i = (ci, b, p)
                x = am[b, p][0:CHUNK] + avs[i]
                uv[b, p] = jnp.concatenate([stacked(_dot(tinv[i], stacked(x))), vs[i]], axis=0)
            yield
            for b, p in pairs:
                yb[b, p] = am[b, p][CHUNK:2 * CHUNK] + _dot(lrk[ci, b, p], uv[b, p])
            yield
            for n, (b, p) in enumerate(pairs):
                rwstate_ref[n] = ((st[b, p] + _dot_tn(uv[b, p], rhs[ci, b, p]))
                                  * elast_ref[b, c][:, p * LANES:(p + 1) * LANES])
            yield
            ys.append(jnp.concatenate(
                [jnp.concatenate([yb[b, p] for p in range(n_pairs)], axis=1) for b in range(nb)], axis=0))
        y = jnp.concatenate(ys, axis=0)
        mean = segsum(y) * (1.0 / HEAD)
        yc = y - mean
        yield
        var = segsum(yc * yc) * (1.0 / HEAD)
        yn = yc * lax.rsqrt(var + RW_GN_EPS) * lnw_ref[...] + lnb_ref[...]
        for ci, (_, rows) in enumerate(chunks):
            store(rows, 0, (yn[ci * rows_all:(ci + 1) * rows_all] + tokens(bonus_ref, rows, 0, rw_w))
                  * _silu(tokens(u_ref, rows, o_rwz, o_rwz + rw_w)))
            yield

    def hgrn_stages(rows):
        q = tokens(u_ref, rows, o_hgq, o_hgq + hg_w)
        f = lb + (1.0 - lb) * _sigmoid(tokens(u_ref, rows, o_hgf, o_hgf + hg_w))
        g = jnp.log2(f)
        kx = 1.0 - f
        iv = tokens(u_ref, rows, o_hgi, o_hgi + hg_w)
        bb = _dot(hgm_ref[...], jnp.concatenate(_split2(g), axis=0))
        part = lambda n: bb[n * rows_all:(n + 1) * rows_all]
        bsub = part(0)
        n_lv = len(HG_LEVELS)
        yield
        bchunk = part(n_lv + 1)
        qc = (q * jnp.exp2(bchunk)).astype(BF16)
        kc = (kx * jnp.exp2(part(n_lv + 2))).astype(BF16)
        ivb = iv.astype(BF16)
        ocs = {}
        for n, (b, p) in enumerate(pairs):
            hst = hgstate_ref[n]
            ocs[b, p] = _dot_nt(blk(qc, b, p), hst.astype(BF16))
            upd = jnp.where(bd_state, _dot_tn(blk(ivb, b, p), blk(kc, b, p)), 0.0)
            last = (b + 1) * CHUNK - 1
            hgstate_ref[n] = hst * jnp.exp2(bchunk[last:last + 1, p * LANES:(p + 1) * LANES]) + upd
        yield
        scs = {i: None for i in pairs}
        for lv, size in enumerate(HG_LEVELS):
            pst = jnp.exp2(part(1 + lv))
            upper = (row & (size - 1)) >= size // 2
            qs = jnp.where(upper, q * pst, 0.0)
            ks = jnp.where(upper, 0.0, kx * pst)
            for i in pairs:
                sc = jnp.where(same_group[size], _dot_nt(blk(qs, *i).astype(BF16), stacked(blk(ks, *i))), 0.0)
                scs[i] = sc if scs[i] is None else scs[i] + sc
            yield
        ostr = {i: _mm(scs[i], stacked(blk(iv, *i))) + ocs[i] for i in pairs}
        yield
        sub = (rows_all // SUBLANES, SUBLANES, hg_w)
        pos = _iota((1, SUBLANES, 1), 1) & (HG_SUB - 1)
        q3, b3, k3, v3 = (z.reshape(sub) for z in (q, bsub, kx, iv))
        wgts = [q * kx]
        for d in range(1, HG_SUB):
            wgt = jnp.where(pos >= d, q3 * jnp.exp2(b3 - pltpu.roll(b3, d, axis=1)) * pltpu.roll(k3, d, axis=1), 0.0)
            wgts.append(wgt.reshape(rows_all, hg_w))
        yield
        sums = segsum(jnp.concatenate(wgts, axis=0))
        yield
        acc = sums[0:rows_all] * iv
        for d in range(1, HG_SUB):
            acc = acc + sums[d * rows_all:(d + 1) * rows_all] * pltpu.roll(v3, d, axis=1).reshape(rows_all, hg_w)
        o = acc + jnp.concatenate(
            [jnp.concatenate([ostr[(b, p)] for p in range(n_pairs)], axis=1) for b in range(nb)], axis=0)
        ms = segsum(o * o) * (1.0 / HEAD)
        store(rows, rw_w, o * lax.rsqrt(ms + NORM_EPS) * hgnw_ref[...] * _silu(tokens(u_ref, rows, o_hgz, o_hgz + hg_w)))

    def lru_stages(rows):
        xb = tokens(u_ref, rows, o_lrx, o_lrx + lru_w)
        row8 = _iota((SUBLANES, 1), 0)
        yv = convb_ref[...] + convw_ref[CONV_WIDTH - 1:CONV_WIDTH, :] * xb
        for d in range(1, CONV_WIDTH):
            rolled = pltpu.roll(xb, d, axis=0)
            pieces = []
            for b in range(nb):
                tail = pltpu.roll(lrux_ref[b], d, axis=0)
                pieces.append(jnp.where(row8 < d, tail, rolled[b * CHUNK:b * CHUNK + SUBLANES]))
                pieces.append(rolled[b * CHUNK + SUBLANES:(b + 1) * CHUNK])
            yv = yv + convw_ref[CONV_WIDTH - 1 - d:CONV_WIDTH - d, :] * jnp.concatenate(pieces, axis=0)
        for b in range(nb):
            lrux_ref[b] = xb[(b + 1) * CHUNK - SUBLANES:(b + 1) * CHUNK]
        yield
        ybf = yv.astype(BF16)
        rg = _sigmoid(_dot(ybf, wa_ref[...]) + ba_ref[...])
        ig = _sigmoid(_dot(ybf, wx_ref[...]) + bx_ref[...])
        log_a = -LRU_C * rg * _softplus(-lam_ref[...])
        av = jnp.exp(log_a)
        th = jnp.tanh(log_a)
        gu = jnp.sqrt(-2.0 * th / (1.0 - th)) * (ig * yv)
        yield
        step = 1
        while step < CHUNK:
            keep = tpos >= step
            a_sh = jnp.where(keep, pltpu.roll(av, step, axis=0), 1.0)
            u_sh = jnp.where(keep, pltpu.roll(gu, step, axis=0), 0.0)
            gu = av * u_sh + gu
            av = av * a_sh
            step *= 2
            yield
        hv = gu + av * per_batch(lruh_ref)
        for b in range(nb):
            lruh_ref[b] = hv[(b + 1) * CHUNK - 1:(b + 1) * CHUNK]
        store(rows, rw_w + hg_w, hv * _silu(tokens(u_ref, rows, o_lrz, o_lrz + lru_w)))

    def project_rows(r0, n_rows):
        mixed = jnp.concatenate([out_ref[b, r0:r0 + n_rows, :] for b in range(nb)], axis=0)
        proj = _dot(mixed, wout_ref[...])
        yield
        for b in range(nb):
            xn = x_ref[b, r0:r0 + n_rows, :] + proj[b * n_rows:(b + 1) * n_rows]
            if final:
                ms = jnp.mean(xn * xn, axis=-1, keepdims=True)
                xn = xn * lax.rsqrt(ms + NORM_EPS) * fw_ref[...]
            xout_ref[b, r0:r0 + n_rows, :] = xn
        yield

    assert n_chunks % CHUNKS_PER_ITER == 0
    group_rows = CHUNKS_PER_ITER * CHUNK
    for gi in range(n_chunks // CHUNKS_PER_ITER):
        chunks = [(gi * CHUNKS_PER_ITER + j, pl.ds((gi * CHUNKS_PER_ITER + j) * CHUNK, CHUNK))
                  for j in range(CHUNKS_PER_ITER)]
        pending = ([rwkv_stages(chunks)] + [hgrn_stages(rows) for _, rows in chunks]
                   + [lru_stages(rows) for _, rows in chunks])
        if gi > 0:
            pending.append(project_rows((gi - 1) * group_rows, group_rows))
        while pending:
            for gen in list(pending):
                if next(gen, True):
                    pending.remove(gen)
    for _ in project_rows((n_chunks // CHUNKS_PER_ITER - 1) * group_rows, group_rows):
        pass


def _mixer_constants(nb):
    rows_all = nb * CHUNK
    t = np.arange(rows_all)
    same_chunk = (t[None, :] // CHUNK) == (t[:, None] // CHUNK)
    lower = t[None, :] <= t[:, None]
    tri2 = np.tile(lower & same_chunk, (1, 2))
    mats = [lower & ((t[None, :] // HG_SUB) == (t[:, None] // HG_SUB))]
    for size in HG_LEVELS:
        half = size // 2
        same = (t[None, :] // size) == (t[:, None] // size)
        upper_row = (t[:, None] % size) >= half
        upper_col = (t[None, :] % size) >= half
        mats.append(same & np.where(upper_row, upper_col & lower, (~upper_col) & (~lower)))
    mats.append(lower & same_chunk)
    mats.append((~lower) & same_chunk)
    hgm = np.tile(np.concatenate(mats, axis=0), (1, 2))
    h = np.arange(2 * LANES) // HEAD
    seg256 = h[None, :] == h[:, None]
    as_bf16 = lambda m: jnp.asarray(m.astype(np.float32), dtype=BF16)
    return as_bf16(tri2), as_bf16(hgm), as_bf16(seg256)


def _block_diag(w):
    nl, g, n, _ = w.shape
    out = jnp.zeros((nl, g * n, g * n), w.dtype)
    for i in range(g):
        out = out.at[:, i * n:(i + 1) * n, i * n:(i + 1) * n].set(w[:, i])
    return out


def _mixer(prep, x2, w_out_bf16, final_w, layer, p, batch, seq):
    pack, vb, bonus, elast, rest = prep
    d = x2.shape[1]
    rw_w = vb.shape[1]
    hg_w = p["hgnw"][0].shape[-1]
    lru_w = p["lam"][0].shape[-1]
    d_mix = rw_w + hg_w + lru_w
    tb = _row_block(seq, 256)
    assert tb % (CHUNKS_PER_ITER * CHUNK) == 0 and rw_w == 3 * LANES and hg_w == rw_w
    assert w_out_bf16[0].shape[-2] == d_mix

    def tok_spec(width):
        return pl.BlockSpec((batch, tb, width), lambda t: (0, t, 0))

    def full_spec(a):
        nd = a.ndim
        return pl.BlockSpec(a.shape, lambda t: (0,) * nd)

    per_token = lambda a: a.reshape(batch, seq, a.shape[1])
    args = [per_token(pack), per_token(vb), per_token(bonus), elast.reshape(batch, seq // CHUNK, 1, rw_w),
            per_token(rest), per_token(x2), w_out_bf16[0]]
    in_specs = [tok_spec(4 * rw_w), tok_spec(rw_w), tok_spec(rw_w),
                pl.BlockSpec((batch, tb // CHUNK, 1, rw_w), lambda t: (0, t, 0, 0)), tok_spec(rest.shape[1]),
                tok_spec(d), _layer_spec(*w_out_bf16)]
    if final_w is not None:
        args.append(final_w)
        in_specs.append(full_spec(final_w))
    names = ["lnw", "lnb", "lbraw", "hgnw", "convw", "convb", "wa", "ba", "wx", "bx", "lam"]
    for k in names:
        a, idx = p[k]
        args.append(a)
        in_specs.append(full_spec(a) if idx is None else _layer_spec(a, idx))
    _, hgm, seg256 = _mixer_constants(batch)
    for a in (hgm, seg256):
        args.append(a)
        in_specs.append(full_spec(a))

    n_pairs = rw_w // LANES
    scratch = [pltpu.VMEM((batch, tb, d_mix), BF16),
               pltpu.VMEM((batch * n_pairs, LANES, LANES), F32),
               pltpu.VMEM((batch * n_pairs, LANES, LANES), F32),
               pltpu.VMEM((batch, SUBLANES, lru_w), F32),
               pltpu.VMEM((batch, 1, lru_w), F32)]
    block_bytes = (batch * tb * (rest.shape[1] * 4 + 4 * rw_w * 2 + rw_w * 2 + rw_w * 4 + d_mix * 2 + 2 * d * 4)
                   + d_mix * d * 2)
    body = functools.partial(_mix_kernel, layer=layer, final=final_w is not None, n_chunks=tb // CHUNK, nb=batch,
                             rw_w=rw_w, hg_w=hg_w, lru_w=lru_w)
    x_new = pl.pallas_call(
        body,
        out_shape=jax.ShapeDtypeStruct((batch, seq, d), F32),
        grid=(seq // tb,),
        in_specs=in_specs,
        out_specs=tok_spec(d),
        scratch_shapes=scratch,
        compiler_params=pltpu.CompilerParams(
            dimension_semantics=("arbitrary",), vmem_limit_bytes=_vmem_limit(block_bytes)),
        name=f"mixer_l{layer}",
    )(*args)
    return x_new.reshape(batch * seq, d)


def kernel(x, norm_w, w_in, rw_mu, rw_w0, rw_w_up, rw_a0, rw_a_up, rw_v0, rw_v_dn, rw_v_up, rw_k_k, rw_k_a, rw_r_k, rw_ln_w, rw_ln_b, hg_lb_raw, hg_norm_w, lru_conv_w, lru_conv_b, lru_wa, lru_ba, lru_wx, lru_bx, lru_lambda, w_out, final_norm_w):
    batch, seq, d = x.shape
    depth = w_in.shape[0]
    rw_w = rw_w0.shape[1]
    lora = rw_w_up.shape[1]
    assert 2 * lora == LANES
    x2 = x.reshape(batch * seq, d)
    zeros_code = jnp.zeros((depth, lora, rw_w), F32)
    per_layer = lambda a: a.reshape(a.shape[0], 1, -1)
    stacked = {
        "norm": per_layer(norm_w), "win": w_in.astype(BF16), "wout": w_out.astype(BF16),
        "mu": per_layer(rw_mu), "w0": per_layer(rw_w0), "a0": per_layer(rw_a0),
        "wcode": jnp.concatenate([jnp.concatenate([rw_w_up, zeros_code], axis=2),
                                  jnp.concatenate([zeros_code, rw_a_up], axis=2)], axis=1).astype(BF16),
        "kk": per_layer(rw_k_k), "ka": per_layer(rw_k_a), "rk": per_layer(rw_r_k),
        "lnw": per_layer(rw_ln_w), "lnb": per_layer(rw_ln_b), "hgnw": per_layer(hg_norm_w),
        "convw": lru_conv_w, "convb": per_layer(lru_conv_b),
        "wa": _block_diag(lru_wa).astype(BF16), "ba": per_layer(lru_ba),
        "wx": _block_diag(lru_wx).astype(BF16), "bx": per_layer(lru_bx), "lam": per_layer(lru_lambda),
    }
    vres = {"v0": per_layer(rw_v0), "vdn": rw_v_dn.astype(BF16), "vup": rw_v_up.astype(BF16)}
    v_first = None
    for l in range(depth):
        p = {k: (a, l) for k, a in stacked.items()}
        p["lbraw"] = (hg_lb_raw, None)
        if l > 0:
            p.update({k: (a, l - 1) for k, a in vres.items()})
        prep = _inproj(x2, p["norm"], p["win"], v_first, p, seq)
        if l == 0:
            v_first = prep[5]
        fw = final_norm_w.reshape(1, -1) if l == depth - 1 else None
        x2 = _mixer(prep[:5], x2, p["wout"], fw, l, p, batch, seq)
    return x2.reshape(batch, seq, d)
```

```python
import functools

import numpy as np
import jax
import jax.numpy as jnp
from jax import lax
from jax.experimental import pallas as pl
from jax.experimental.pallas import tpu as pltpu

F32 = jnp.float32
BF16 = jnp.bfloat16

NORM_EPS = 1e-6
RW_GN_EPS = 64e-5
LRU_C = 8.0
HEAD = 64
LANES = 128
SUBLANES = 8
CHUNK = 64
HG_SUB = 4
HG_LEVELS = (8, 16, 32, 64)
CONV_WIDTH = 4
LOG2_HEAD = HEAD.bit_length() - 1
LOG2_E = 1.4426950408889634
CHUNKS_PER_ITER = 2
V7X_VMEM_BYTES = 64 * 1024 * 1024


def _dot(a, b):
    return jnp.dot(a, b, preferred_element_type=F32)


def _dot_nt(a, b):
    return lax.dot_general(a, b, (((1,), (1,)), ((), ())), preferred_element_type=F32)


def _dot_tn(a, b):
    return lax.dot_general(a, b, (((0,), (0,)), ((), ())), preferred_element_type=F32)


def _mm(a, b):
    return _dot(a.astype(BF16), b.astype(BF16))


def _split2(x):
    hi = x.astype(BF16)
    lo = (x - hi.astype(F32)).astype(BF16)
    return hi, lo


def _sigmoid(x):
    return 1.0 / (1.0 + jnp.exp(-x))


def _silu(x):
    return x * _sigmoid(x)


def _softplus(x):
    return jnp.maximum(x, 0.0) + jnp.log1p(jnp.exp(-jnp.abs(x)))


def _iota(shape, dim):
    return lax.broadcasted_iota(jnp.int32, shape, dim)


def _head_sums(x, seg256):
    xb = x.astype(BF16)
    seg128 = seg256[0:LANES, 0:LANES]
    return jnp.concatenate([_dot(xb[:, 0:2 * LANES], seg256), _dot(xb[:, 2 * LANES:3 * LANES], seg128)], axis=1)


def _inproj_kernel(*refs, has_vres, rw_w, tiles_per_seq):
    it = iter(refs)
    x_ref, nw_ref, w_ref = (next(it) for _ in range(3))
    vfirst_in_ref = next(it) if has_vres else None
    mu_ref, w0_ref, a0_ref, wcode_ref = (next(it) for _ in range(4))
    if has_vres:
        v0_ref, vdn_ref, vup_ref = (next(it) for _ in range(3))
    kk_ref, ka_ref, rk_ref, tri2_ref, seg256_ref = (next(it) for _ in range(5))
    pack_ref, v_ref, bonus_ref, elast_ref, rest_ref = (next(it) for _ in range(5))
    vfirst_out_ref = None if has_vres else next(it)
    prev_ref = next(it)

    tm = x_ref.shape[0]
    sub = 2 * CHUNK
    n_sub = tm // sub
    rw_shift = 3 * rw_w + LANES
    rest_tiles = (w_ref.shape[1] - rw_shift) // LANES

    @pl.when(lax.rem(pl.program_id(0), tiles_per_seq) == 0)
    def _new_sequence():
        prev_ref[...] = jnp.zeros_like(prev_ref)

    x = x_ref[...]
    ms = jnp.mean(x * x, axis=-1, keepdims=True)
    hb = (x * lax.rsqrt(ms + NORM_EPS) * nw_ref[...]).astype(BF16)
    u_rw = _dot(hb, w_ref[:, 0:rw_shift])

    seg256 = seg256_ref[...]
    row = _iota((sub, 1), 0)
    lane_first_half = _iota((1, LANES), 1) < LANES // 2
    prev = prev_ref[...]
    for s in range(n_sub):
        lo = rw_shift + (rest_tiles * s // n_sub) * LANES
        hi = rw_shift + (rest_tiles * (s + 1) // n_sub) * LANES
        rest_ref[:, lo - rw_shift:hi - rw_shift] = _dot(hb, w_ref[:, lo:hi])

        rs = slice(s * sub, (s + 1) * sub)
        ru = u_rw[rs]
        sh = jnp.where(row == 0, prev, pltpu.roll(ru, 1, axis=0))
        prev = ru[sub - 1:sub, :]
        ul = ru + mu_ref[...] * (sh - ru)
        r = ul[:, 0:rw_w]
        k = ul[:, rw_w:2 * rw_w]
        v = ul[:, 2 * rw_w:3 * rw_w]
        codes = ul[:, 3 * rw_w:rw_shift]
        both = _mm(jnp.where(lane_first_half, jnp.tanh(codes), codes), wcode_ref[...])
        w_log = -_softplus(-(w0_ref[...] + both[:, 0:rw_w])) - 0.5
        nlw2 = jnp.exp(w_log) * LOG2_E
        a = _sigmoid(a0_ref[...] + both[:, rw_w:2 * rw_w])
        if has_vres:
            gate = _sigmoid(v0_ref[...] + _mm(_mm(v, vdn_ref[...]), vup_ref[...]))
            v = v + (vfirst_in_ref[rs, :] - v) * gate
        else:
            vfirst_out_ref[rs, :] = v
        kk = k * kk_ref[...]
        k2 = k * (1.0 + (a - 1.0) * ka_ref[...])
        sums = _head_sums(jnp.concatenate([kk * kk, r * k2 * rk_ref[...]], axis=0), seg256)
        kk = kk / jnp.maximum(jnp.sqrt(sums[0:sub]), 1e-12)
        bonus_ref[rs, :] = sums[sub:2 * sub] * v
        v_ref[rs, :] = v.astype(BF16)
        cs = _dot(tri2_ref[...], jnp.concatenate(_split2(nlw2), axis=0))
        e_pos = jnp.exp2(-cs)
        e_neg = jnp.exp2(cs)
        pack_ref[rs, 0:rw_w] = (-kk * jnp.exp2(nlw2 - cs)).astype(BF16)
        pack_ref[rs, rw_w:2 * rw_w] = (r * e_pos).astype(BF16)
        pack_ref[rs, 2 * rw_w:3 * rw_w] = (kk * a * e_neg).astype(BF16)
        pack_ref[rs, 3 * rw_w:4 * rw_w] = (k2 * e_neg).astype(BF16)
        for c in range(sub // CHUNK):
            n = s * (sub // CHUNK) + c
            elast_ref[n:n + 1, :] = e_pos[(c + 1) * CHUNK - 1:(c + 1) * CHUNK, :]
    prev_ref[...] = prev


def _row_block(n_rows, target):
    blk = min(target, n_rows)
    while n_rows % blk:
        blk //= 2
    return blk


def _vmem_limit(block_bytes):
    return int(min(V7X_VMEM_BYTES * 15 // 16, 2 * block_bytes + 16 * 1024 * 1024))


def _layer_spec(a, idx):
    nd = a.ndim
    return pl.BlockSpec((None,) + a.shape[1:], lambda *_: (idx,) + (0,) * (nd - 1))


def _inproj(x2, norm_w, w_bf16, v_first, p, seq):
    n, d = x2.shape
    n_in = w_bf16[0].shape[-1]
    rw_w = p["w0"][0].shape[-1]
    rw_shift = 3 * rw_w + LANES
    tm = _row_block(seq, 512)
    assert tm % (SUBLANES * CHUNK) == 0 and seq % tm == 0
    has_vres = v_first is not None

    def tok_spec(width):
        return pl.BlockSpec((tm, width), lambda i: (i, 0))

    def full_spec(a):
        nd = a.ndim
        return pl.BlockSpec(a.shape, lambda i: (0,) * nd)

    tri2, _, seg256 = _mixer_constants(2)
    args, in_specs = [x2], [tok_spec(d)]
    for a, idx in (norm_w, w_bf16):
        args.append(a)
        in_specs.append(_layer_spec(a, idx))
    if has_vres:
        args.append(v_first)
        in_specs.append(tok_spec(rw_w))
    names = ["mu", "w0", "a0", "wcode"] + (["v0", "vdn", "vup"] if has_vres else []) + ["kk", "ka", "rk"]
    for a, idx in [p[k] for k in names]:
        args.append(a)
        in_specs.append(_layer_spec(a, idx))
    for a in (tri2, seg256):
        args.append(a)
        in_specs.append(full_spec(a))
    out_shape = [jax.ShapeDtypeStruct((n, 4 * rw_w), BF16), jax.ShapeDtypeStruct((n, rw_w), BF16),
                 jax.ShapeDtypeStruct((n, rw_w), F32), jax.ShapeDtypeStruct((n // CHUNK, rw_w), F32),
                 jax.ShapeDtypeStruct((n, n_in - rw_shift), F32)]
    out_specs = [tok_spec(4 * rw_w), tok_spec(rw_w), tok_spec(rw_w),
                 pl.BlockSpec((tm // CHUNK, rw_w), lambda i: (i, 0)), tok_spec(n_in - rw_shift)]
    if not has_vres:
        out_shape.append(jax.ShapeDtypeStruct((n, rw_w), F32))
        out_specs.append(tok_spec(rw_w))
    block_bytes = tm * d * 4 + d * n_in * 2 + tm * (n_in + 4 * rw_w) * 4
    body = functools.partial(_inproj_kernel, has_vres=has_vres, rw_w=rw_w, tiles_per_seq=seq // tm)
    return pl.pallas_call(
        body,
        out_shape=out_shape,
        grid=(n // tm,),
        in_specs=in_specs,
        out_specs=out_specs,
        scratch_shapes=[pltpu.VMEM((1, rw_shift), F32)],
        compiler_params=pltpu.CompilerParams(
            dimension_semantics=("arbitrary",), vmem_limit_bytes=_vmem_limit(block_bytes)),
        name="inproj_vres" if has_vres else "inproj",
    )(*args)


def _stack_heads(x, lane_head0):
    return jnp.concatenate([jnp.where(lane_head0, x, 0.0), jnp.where(lane_head0, 0.0, x)], axis=0)


def _mix_kernel(*refs, layer, final, n_chunks, nb, rw_w, hg_w, lru_w):
    it = iter(refs)
    pack_ref, v_ref, bonus_ref, elast_ref, u_ref, x_ref, wout_ref = (next(it) for _ in range(7))
    fw_ref = next(it) if final else None
    lnw_ref, lnb_ref, lbraw_ref, hgnw_ref = (next(it) for _ in range(4))
    convw_ref, convb_ref, wa_ref, ba_ref, wx_ref, bx_ref, lam_ref = (next(it) for _ in range(7))
    hgm_ref, seg256_ref = (next(it) for _ in range(2))
    xout_ref = next(it)
    out_ref, rwstate_ref, hgstate_ref, lrux_ref, lruh_ref = (next(it) for _ in range(5))

    n_pairs = rw_w // LANES
    rows_all = nb * CHUNK
    o_rwz = 0
    o_hgq = o_rwz + rw_w
    o_hgf = o_hgq + hg_w
    o_hgi = o_hgf + hg_w
    o_hgz = o_hgi + hg_w
    o_lrx = o_hgz + hg_w
    o_lrz = o_lrx + lru_w

    @pl.when(pl.program_id(0) == 0)
    def _reset():
        rwstate_ref[...] = jnp.zeros_like(rwstate_ref)
        hgstate_ref[...] = jnp.zeros_like(hgstate_ref)
        lrux_ref[...] = jnp.zeros_like(lrux_ref)
        lruh_ref[...] = jnp.zeros_like(lruh_ref)

    row = _iota((rows_all, 1), 0)
    tpos = row & (CHUNK - 1)
    lane_head0 = _iota((1, LANES), 1) < HEAD
    ti = _iota((CHUNK, LANES), 0)
    tj = _iota((CHUNK, LANES), 1) & (CHUNK - 1)
    strict_lower = tj < ti
    incl_lower = tj <= ti
    eye = jnp.where(tj == ti, 1.0, 0.0)
    same_group = {size: (ti >> (size.bit_length() - 1)) == (tj >> (size.bit_length() - 1)) for size in HG_LEVELS}
    bd_state = (_iota((LANES, LANES), 0) >> LOG2_HEAD) == (_iota((LANES, LANES), 1) >> LOG2_HEAD)
    pairs = [(b, p) for p in range(n_pairs) for b in range(nb)]

    seg256 = seg256_ref[...]

    def segsum(x):
        return _head_sums(x, seg256)

    def stacked(m):
        return _stack_heads(m, lane_head0).astype(BF16)

    def per_batch(ref):
        val = ref[0]
        for b in range(1, nb):
            val = jnp.where(row >= b * CHUNK, ref[b], val)
        return val

    def tokens(ref, rows, lo, hi):
        return jnp.concatenate([ref[b, rows, lo:hi] for b in range(nb)], axis=0)

    def store(rows, lo, val):
        for b in range(nb):
            out_ref[b, rows, lo:lo + val.shape[1]] = val[b * CHUNK:(b + 1) * CHUNK].astype(out_ref.dtype)

    def blk(x, b, p):
        return x[b * CHUNK:(b + 1) * CHUNK, p * LANES:(p + 1) * LANES]

    lbraw = lbraw_ref[...]
    lbe = jnp.exp(lbraw - jnp.max(lbraw, axis=0, keepdims=True))
    lbw = lbe / jnp.sum(lbe, axis=0, keepdims=True)
    lb = lbw[0:1, :]
    for j in range(1, layer + 1):
        lb = lb + lbw[j:j + 1, :]
    lb = lb - lbw[0:1, :]

    def rwkv_stages(chunks):
        insts = [(ci, b, p) for ci in range(len(chunks)) for (b, p) in pairs]
        lhs, rhs, vs, a_ak, lrk, avs, pw, tinv = {}, {}, {}, {}, {}, {}, {}, {}
        for ci, (_, rows) in enumerate(chunks):
            a_t = tokens(pack_ref, rows, 0, rw_w)
            r_t = tokens(pack_ref, rows, rw_w, 2 * rw_w)
            b_t = tokens(pack_ref, rows, 2 * rw_w, 3 * rw_w)
            k_t = tokens(pack_ref, rows, 3 * rw_w, 4 * rw_w)
            v = tokens(v_ref, rows, 0, rw_w)
            for b, p in pairs:
                i = (ci, b, p)
                lhs[i] = jnp.concatenate([blk(a_t, b, p), blk(r_t, b, p)], axis=0)
                rhs[i] = jnp.concatenate([_stack_heads(blk(b_t, b, p), lane_head0),
                                          _stack_heads(blk(k_t, b, p), lane_head0)], axis=0)
                vs[i] = _stack_heads(blk(v, b, p), lane_head0)
        yield
        for i in insts:
            sc = _dot_nt(lhs[i], rhs[i])
            pw[i] = jnp.where(strict_lower, sc[0:CHUNK, 0:LANES], 0.0)
            a_ak[i] = jnp.where(strict_lower, sc[0:CHUNK, LANES:2 * LANES], 0.0).astype(BF16)
            lrk[i] = jnp.concatenate([jnp.where(incl_lower, sc[CHUNK:2 * CHUNK, 0:LANES], 0.0),
                                      jnp.where(incl_lower, sc[CHUNK:2 * CHUNK, LANES:2 * LANES], 0.0)],
                                     axis=1).astype(BF16)
        yield
        for i in insts:
            avs[i] = _dot(a_ak[i], vs[i])
            tinv[i] = eye + pw[i]
            pw[i] = _dot(pw[i].astype(BF16), stacked(pw[i]))
        yield
        for s in range(1, 5):
            for i in insts:
                both = _dot(jnp.concatenate([pw[i], tinv[i]], axis=0).astype(BF16), stacked(pw[i]))
                pw[i] = both[0:CHUNK]
                tinv[i] = tinv[i] + both[CHUNK:2 * CHUNK]
            yield
        for i in insts:
            tinv[i] = (tinv[i] + _dot(tinv[i].astype(BF16), stacked(pw[i]))).astype(BF16)
        yield
        ys = []
        for ci, (c, rows) in enumerate(chunks):
            st, am, uv, yb = {}, {}, {}, {}
            for n, (b, p) in enumerate(pairs):
                st[b, p] = rwstate_ref[n]
                am[b, p] = _dot_nt(lhs[ci, b, p], st[b, p].astype(BF16))
            yield
            for b, p in pairs:
                i = (ci, b, p)
                x = am[b, p][0:CHUNK] + avs[i]
                uv[b, p] = jnp.concatenate([stacked(_dot(tinv[i], stacked(x))), vs[i]], axis=0)
            yield
            for b, p in pairs:
                yb[b, p] = am[b, p][CHUNK:2 * CHUNK] + _dot(lrk[ci, b, p], uv[b, p])
            yield
            for n, (b, p) in enumerate(pairs):
                rwstate_ref[n] = ((st[b, p] + _dot_tn(uv[b, p], rhs[ci, b, p]))
                                  * elast_ref[b, c][:, p * LANES:(p + 1) * LANES])
            yield
            ys.append(jnp.concatenate(
                [jnp.concatenate([yb[b, p] for p in range(n_pairs)], axis=1) for b in range(nb)], axis=0))
        y = jnp.concatenate(ys, axis=0)
        mean = segsum(y) * (1.0 / HEAD)
        yc = y - mean
        yield
        var = segsum(yc * yc) * (1.0 / HEAD)
        yn = yc * lax.rsqrt(var + RW_GN_EPS) * lnw_ref[...] + lnb_ref[...]
        for ci, (_, rows) in enumerate(chunks):
            store(rows, 0, (yn[ci * rows_all:(ci + 1) * rows_all] + tokens(bonus_ref, rows, 0, rw_w))
                  * _silu(tokens(u_ref, rows, o_rwz, o_rwz + rw_w)))
            yield

    def hgrn_stages(rows):
        q = tokens(u_ref, rows, o_hgq, o_hgq + hg_w)
        f = lb + (1.0 - lb) * _sigmoid(tokens(u_ref, rows, o_hgf, o_hgf + hg_w))
        g = jnp.log2(f)
        kx = 1.0 - f
        iv = tokens(u_ref, rows, o_hgi, o_hgi + hg_w)
        bb = _dot(hgm_ref[...], jnp.concatenate(_split2(g), axis=0))
        part = lambda n: bb[n * rows_all:(n + 1) * rows_all]
        bsub = part(0)
        n_lv = len(HG_LEVELS)
        yield
        bchunk = part(n_lv + 1)
        qc = (q * jnp.exp2(bchunk)).astype(BF16)
        kc = (kx * jnp.exp2(part(n_lv + 2))).astype(BF16)
        ivb = iv.astype(BF16)
        ocs = {}
        for n, (b, p) in enumerate(pairs):
            hst = hgstate_ref[n]
            ocs[b, p] = _dot_nt(blk(qc, b, p), hst.astype(BF16))
            upd = jnp.where(bd_state, _dot_tn(blk(ivb, b, p), blk(kc, b, p)), 0.0)
            last = (b + 1) * CHUNK - 1
            hgstate_ref[n] = hst * jnp.exp2(bchunk[last:last + 1, p * LANES:(p + 1) * LANES]) + upd
        yield
        scs = {i: None for i in pairs}
        for lv, size in enumerate(HG_LEVELS):
            pst = jnp.exp2(part(1 + lv))
            upper = (row & (size - 1)) >= size // 2
            qs = jnp.where(upper, q * pst, 0.0)
            ks = jnp.where(upper, 0.0, kx * pst)
            for i in pairs:
                sc = jnp.where(same_group[size], _dot_nt(blk(qs, *i).astype(BF16), stacked(blk(ks, *i))), 0.0)
                scs[i] = sc if scs[i] is None else scs[i] + sc
            yield
        ostr = {i: _mm(scs[i], stacked(blk(iv, *i))) + ocs[i] for i in pairs}
        yield
        sub = (rows_all // SUBLANES, SUBLANES, hg_w)
        pos = _iota((1, SUBLANES, 1), 1) & (HG_SUB - 1)
        q3, b3, k3, v3 = (z.reshape(sub) for z in (q, bsub, kx, iv))
        wgts = [q * kx]
        for d in range(1, HG_SUB):
            wgt = jnp.where(pos >= d, q3 * jnp.exp2(b3 - pltpu.roll(b3, d, axis=1)) * pltpu.roll(k3, d, axis=1), 0.0)
            wgts.append(wgt.reshape(rows_all, hg_w))
        yield
        sums = segsum(jnp.concatenate(wgts, axis=0))
        yield
        acc = sums[0:rows_all] * iv
        for d in range(1, HG_SUB):
            acc = acc + sums[d * rows_all:(d + 1) * rows_all] * pltpu.roll(v3, d, axis=1).reshape(rows_all, hg_w)
        o = acc + jnp.concatenate(
            [jnp.concatenate([ostr[(b, p)] for p in range(n_pairs)], axis=1) for b in range(nb)], axis=0)
        ms = segsum(o * o) * (1.0 / HEAD)
        store(rows, rw_w, o * lax.rsqrt(ms + NORM_EPS) * hgnw_ref[...] * _silu(tokens(u_ref, rows, o_hgz, o_hgz + hg_w)))

    def lru_stages(rows):
        xb = tokens(u_ref, rows, o_lrx, o_lrx + lru_w)
        row8 = _iota((SUBLANES, 1), 0)
        yv = convb_ref[...] + convw_ref[CONV_WIDTH - 1:CONV_WIDTH, :] * xb
        for d in range(1, CONV_WIDTH):
            rolled = pltpu.roll(xb, d, axis=0)
            pieces = []
            for b in range(nb):
                tail = pltpu.roll(lrux_ref[b], d, axis=0)
                pieces.append(jnp.where(row8 < d, tail, rolled[b * CHUNK:b * CHUNK + SUBLANES]))
                pieces.append(rolled[b * CHUNK + SUBLANES:(b + 1) * CHUNK])
            yv = yv + convw_ref[CONV_WIDTH - 1 - d:CONV_WIDTH - d, :] * jnp.concatenate(pieces, axis=0)
        for b in range(nb):
            lrux_ref[b] = xb[(b + 1) * CHUNK - SUBLANES:(b + 1) * CHUNK]
        yield
        ybf = yv.astype(BF16)
        rg = _sigmoid(_dot(ybf, wa_ref[...]) + ba_ref[...])
        ig = _sigmoid(_dot(ybf, wx_ref[...]) + bx_ref[...])
        log_a = -LRU_C * rg * _softplus(-lam_ref[...])
        av = jnp.exp(log_a)
        th = jnp.tanh(log_a)
        gu = jnp.sqrt(-2.0 * th / (1.0 - th)) * (ig * yv)
        yield
        step = 1
        while step < CHUNK:
            keep = tpos >= step
            a_sh = jnp.where(keep, pltpu.roll(av, step, axis=0), 1.0)
            u_sh = jnp.where(keep, pltpu.roll(gu, step, axis=0), 0.0)
            gu = av * u_sh + gu
            av = av * a_sh
            step *= 2
            yield
        hv = gu + av * per_batch(lruh_ref)
        for b in range(nb):
            lruh_ref[b] = hv[(b + 1) * CHUNK - 1:(b + 1) * CHUNK]
        store(rows, rw_w + hg_w, hv * _silu(tokens(u_ref, rows, o_lrz, o_lrz + lru_w)))

    def project_rows(r0, n_rows):
        mixed = jnp.concatenate([out_ref[b, r0:r0 + n_rows, :] for b in range(nb)], axis=0)
        proj = _dot(mixed, wout_ref[...])
        yield
        for b in range(nb):
            xn = x_ref[b, r0:r0 + n_rows, :] + proj[b * n_rows:(b + 1) * n_rows]
            if final:
                ms = jnp.mean(xn * xn, axis=-1, keepdims=True)
                xn = xn * lax.rsqrt(ms + NORM_EPS) * fw_ref[...]
            xout_ref[b, r0:r0 + n_rows, :] = xn
        yield

    assert n_chunks % CHUNKS_PER_ITER == 0
    group_rows = CHUNKS_PER_ITER * CHUNK
    for gi in range(n_chunks // CHUNKS_PER_ITER):
        chunks = [(gi * CHUNKS_PER_ITER + j, pl.ds((gi * CHUNKS_PER_ITER + j) * CHUNK, CHUNK))
                  for j in range(CHUNKS_PER_ITER)]
        pending = ([rwkv_stages(chunks)] + [hgrn_stages(rows) for _, rows in chunks]
                   + [lru_stages(rows) for _, rows in chunks])
        if gi > 0:
            pending.append(project_rows((gi - 1) * group_rows, group_rows))
        while pending:
            for gen in list(pending):
                if next(gen, True):
                    pending.remove(gen)
    for _ in project_rows((n_chunks // CHUNKS_PER_ITER - 1) * group_rows, group_rows):
        pass


def _mixer_constants(nb):
    rows_all = nb * CHUNK
    t = np.arange(rows_all)
    same_chunk = (t[None, :] // CHUNK) == (t[:, None] // CHUNK)
    lower = t[None, :] <= t[:, None]
    tri2 = np.tile(lower & same_chunk, (1, 2))
    mats = [lower & ((t[None, :] // HG_SUB) == (t[:, None] // HG_SUB))]
    for size in HG_LEVELS:
        half = size // 2
        same = (t[None, :] // size) == (t[:, None] // size)
        upper_row = (t[:, None] % size) >= half
        upper_col = (t[None, :] % size) >= half
        mats.append(same & np.where(upper_row, upper_col & lower, (~upper_col) & (~lower)))
    mats.append(lower & same_chunk)
    mats.append((~lower) & same_chunk)
    hgm = np.tile(np.concatenate(mats, axis=0), (1, 2))
    h = np.arange(2 * LANES) // HEAD
    seg256 = h[None, :] == h[:, None]
    as_bf16 = lambda m: jnp.asarray(m.astype(np.float32), dtype=BF16)
    return as_bf16(tri2), as_bf16(hgm), as_bf16(seg256)


def _block_diag(w):
    nl, g, n, _ = w.shape
    out = jnp.zeros((nl, g * n, g * n), w.dtype)
    for i in range(g):
        out = out.at[:, i * n:(i + 1) * n, i * n:(i + 1) * n].set(w[:, i])
    return out


def _mixer(prep, x2, w_out_bf16, final_w, layer, p, batch, seq):
    pack, vb, bonus, elast, rest = prep
    d = x2.shape[1]
    rw_w = vb.shape[1]
    hg_w = p["hgnw"][0].shape[-1]
    lru_w = p["lam"][0].shape[-1]
    d_mix = rw_w + hg_w + lru_w
    tb = _row_block(seq, 512)
    assert tb % (CHUNKS_PER_ITER * CHUNK) == 0 and rw_w == 3 * LANES and hg_w == rw_w
    assert w_out_bf16[0].shape[-2] == d_mix

    def tok_spec(width):
        return pl.BlockSpec((batch, tb, width), lambda t: (0, t, 0))

    def full_spec(a):
        nd = a.ndim
        return pl.BlockSpec(a.shape, lambda t: (0,) * nd)

    per_token = lambda a: a.reshape(batch, seq, a.shape[1])
    args = [per_token(pack), per_token(vb), per_token(bonus), elast.reshape(batch, seq // CHUNK, 1, rw_w),
            per_token(rest), per_token(x2), w_out_bf16[0]]
    in_specs = [tok_spec(4 * rw_w), tok_spec(rw_w), tok_spec(rw_w),
                pl.BlockSpec((batch, tb // CHUNK, 1, rw_w), lambda t: (0, t, 0, 0)), tok_spec(rest.shape[1]),
                tok_spec(d), _layer_spec(*w_out_bf16)]
    if final_w is not None:
        args.append(final_w)
        in_specs.append(full_spec(final_w))
    names = ["lnw", "lnb", "lbraw", "hgnw", "convw", "convb", "wa", "ba", "wx", "bx", "lam"]
    for k in names:
        a, idx = p[k]
        args.append(a)
        in_specs.append(full_spec(a) if idx is None else _layer_spec(a, idx))
    _, hgm, seg256 = _mixer_constants(batch)
    for a in (hgm, seg256):
        args.append(a)
        in_specs.append(full_spec(a))

    n_pairs = rw_w // LANES
    scratch = [pltpu.VMEM((batch, tb, d_mix), BF16),
               pltpu.VMEM((batch * n_pairs, LANES, LANES), F32),
               pltpu.VMEM((batch * n_pairs, LANES, LANES), F32),
               pltpu.VMEM((batch, SUBLANES, lru_w), F32),
               pltpu.VMEM((batch, 1, lru_w), F32)]
    block_bytes = (batch * tb * (rest.shape[1] * 4 + 4 * rw_w * 2 + rw_w * 2 + rw_w * 4 + d_mix * 2 + 2 * d * 4)
                   + d_mix * d * 2)
    body = functools.partial(_mix_kernel, layer=layer, final=final_w is not None, n_chunks=tb // CHUNK, nb=batch,
                             rw_w=rw_w, hg_w=hg_w, lru_w=lru_w)
    x_new = pl.pallas_call(
        body,
        out_shape=jax.ShapeDtypeStruct((batch, seq, d), F32),
        grid=(seq // tb,),
        in_specs=in_specs,
        out_specs=tok_spec(d),
        scratch_shapes=scratch,
        compiler_params=pltpu.CompilerParams(
            dimension_semantics=("arbitrary",), vmem_limit_bytes=_vmem_limit(block_bytes)),
        name=f"mixer_l{layer}",
    )(*args)
    return x_new.reshape(batch * seq, d)


def kernel(x, norm_w, w_in, rw_mu, rw_w0, rw_w_up, rw_a0, rw_a_up, rw_v0, rw_v_dn, rw_v_up, rw_k_k, rw_k_a, rw_r_k, rw_ln_w, rw_ln_b, hg_lb_raw, hg_norm_w, lru_conv_w, lru_conv_b, lru_wa, lru_ba, lru_wx, lru_bx, lru_lambda, w_out, final_norm_w):
    batch, seq, d = x.shape
    depth = w_in.shape[0]
    rw_w = rw_w0.shape[1]
    lora = rw_w_up.shape[1]
    assert 2 * lora == LANES
    x2 = x.reshape(batch * seq, d)
    zeros_code = jnp.zeros((depth, lora, rw_w), F32)
    per_layer = lambda a: a.reshape(a.shape[0], 1, -1)
    stacked = {
        "norm": per_layer(norm_w), "win": w_in.astype(BF16), "wout": w_out.astype(BF16),
        "mu": per_layer(rw_mu), "w0": per_layer(rw_w0), "a0": per_layer(rw_a0),
        "wcode": jnp.concatenate([jnp.concatenate([rw_w_up, zeros_code], axis=2),
                                  jnp.concatenate([zeros_code, rw_a_up], axis=2)], axis=1).astype(BF16),
        "kk": per_layer(rw_k_k), "ka": per_layer(rw_k_a), "rk": per_layer(rw_r_k),
        "lnw": per_layer(rw_ln_w), "lnb": per_layer(rw_ln_b), "hgnw": per_layer(hg_norm_w),
        "convw": lru_conv_w, "convb": per_layer(lru_conv_b),
        "wa": _block_diag(lru_wa).astype(BF16), "ba": per_layer(lru_ba),
        "wx": _block_diag(lru_wx).astype(BF16), "bx": per_layer(lru_bx), "lam": per_layer(lru_lambda),
    }
    vres = {"v0": per_layer(rw_v0), "vdn": rw_v_dn.astype(BF16), "vup": rw_v_up.astype(BF16)}
    v_first = None
    for l in range(depth):
        p = {k: (a, l) for k, a in stacked.items()}
        p["lbraw"] = (hg_lb_raw, None)
        if l > 0:
            p.update({k: (a, l - 1) for k, a in vres.items()})
        prep = _inproj(x2, p["norm"], p["win"], v_first, p, seq)
        if l == 0:
            v_first = prep[5]
        fw = final_norm_w.reshape(1, -1) if l == depth - 1 else None
        x2 = _mixer(prep[:5], x2, p["wout"], fw, l, p, batch, seq)
    return x2.reshape(batch, seq, d)
```

```python
import functools

import numpy as np
import jax
import jax.numpy as jnp
from jax import lax
from jax.experimental import pallas as pl
from jax.experimental.pallas import tpu as pltpu

F32 = jnp.float32
BF16 = jnp.bfloat16

NORM_EPS = 1e-6
RW_GN_EPS = 64e-5
LRU_C = 8.0
HEAD = 64
LANES = 128
SUBLANES = 8
CHUNK = 64
HG_SUB = 4
HG_LEVELS = (8, 16, 32, 64)
CONV_WIDTH = 4
LOG2_HEAD = HEAD.bit_length() - 1
LOG2_E = 1.4426950408889634
CHUNKS_PER_ITER = 2
V7X_VMEM_BYTES = 64 * 1024 * 1024


def _dot(a, b):
    return jnp.dot(a, b, preferred_element_type=F32)


def _dot_nt(a, b):
    return lax.dot_general(a, b, (((1,), (1,)), ((), ())), preferred_element_type=F32)


def _dot_tn(a, b):
    return lax.dot_general(a, b, (((0,), (0,)), ((), ())), preferred_element_type=F32)


def _mm(a, b):
    return _dot(a.astype(BF16), b.astype(BF16))


def _split2(x):
    hi = x.astype(BF16)
    lo = (x - hi.astype(F32)).astype(BF16)
    return hi, lo


def _sigmoid(x):
    return 1.0 / (1.0 + jnp.exp(-x))


def _silu(x):
    return x * _sigmoid(x)


def _softplus(x):
    return jnp.maximum(x, 0.0) + jnp.log1p(jnp.exp(-jnp.abs(x)))


def _iota(shape, dim):
    return lax.broadcasted_iota(jnp.int32, shape, dim)


def _head_sums(x, seg256):
    xb = x.astype(BF16)
    seg128 = seg256[0:LANES, 0:LANES]
    return jnp.concatenate([_dot(xb[:, 0:2 * LANES], seg256), _dot(xb[:, 2 * LANES:3 * LANES], seg128)], axis=1)


def _inproj_kernel(*refs, has_vres, rw_w, tiles_per_seq):
    it = iter(refs)
    x_ref, nw_ref, w_ref = (next(it) for _ in range(3))
    vfirst_in_ref = next(it) if has_vres else None
    mu_ref, w0_ref, a0_ref, wcode_ref = (next(it) for _ in range(4))
    if has_vres:
        v0_ref, vdn_ref, vup_ref = (next(it) for _ in range(3))
    kk_ref, ka_ref, rk_ref, tri2_ref, seg256_ref = (next(it) for _ in range(5))
    pack_ref, v_ref, bonus_ref, elast_ref, rest_ref = (next(it) for _ in range(5))
    vfirst_out_ref = None if has_vres else next(it)
    prev_ref = next(it)

    tm = x_ref.shape[0]
    sub = 2 * CHUNK
    n_sub = tm // sub
    rw_shift = 3 * rw_w + LANES
    rest_tiles = (w_ref.shape[1] - rw_shift) // LANES

    @pl.when(lax.rem(pl.program_id(0), tiles_per_seq) == 0)
    def _new_sequence():
        prev_ref[...] = jnp.zeros_like(prev_ref)

    x = x_ref[...]
    ms = jnp.mean(x * x, axis=-1, keepdims=True)
    hb = (x * lax.rsqrt(ms + NORM_EPS) * nw_ref[...]).astype(BF16)
    u_rw = _dot(hb, w_ref[:, 0:rw_shift])

    seg256 = seg256_ref[...]
    row = _iota((sub, 1), 0)
    lane_first_half = _iota((1, LANES), 1) < LANES // 2
    prev = prev_ref[...]
    for s in range(n_sub):
        lo = rw_shift + (rest_tiles * s // n_sub) * LANES
        hi = rw_shift + (rest_tiles * (s + 1) // n_sub) * LANES
        rest_ref[:, lo - rw_shift:hi - rw_shift] = _dot(hb, w_ref[:, lo:hi])

        rs = slice(s * sub, (s + 1) * sub)
        ru = u_rw[rs]
        sh = jnp.where(row == 0, prev, pltpu.roll(ru, 1, axis=0))
        prev = ru[sub - 1:sub, :]
        ul = ru + mu_ref[...] * (sh - ru)
        r = ul[:, 0:rw_w]
        k = ul[:, rw_w:2 * rw_w]
        v = ul[:, 2 * rw_w:3 * rw_w]
        codes = ul[:, 3 * rw_w:rw_shift]
        both = _mm(jnp.where(lane_first_half, jnp.tanh(codes), codes), wcode_ref[...])
        w_log = -_softplus(-(w0_ref[...] + both[:, 0:rw_w])) - 0.5
        nlw2 = jnp.exp(w_log) * LOG2_E
        a = _sigmoid(a0_ref[...] + both[:, rw_w:2 * rw_w])
        if has_vres:
            gate = _sigmoid(v0_ref[...] + _mm(_mm(v, vdn_ref[...]), vup_ref[...]))
            v = v + (vfirst_in_ref[rs, :] - v) * gate
        else:
            vfirst_out_ref[rs, :] = v
        kk = k * kk_ref[...]
        k2 = k * (1.0 + (a - 1.0) * ka_ref[...])
        sums = _head_sums(jnp.concatenate([kk * kk, r * k2 * rk_ref[...]], axis=0), seg256)
        kk = kk / jnp.maximum(jnp.sqrt(sums[0:sub]), 1e-12)
        bonus_ref[rs, :] = sums[sub:2 * sub] * v
        v_ref[rs, :] = v.astype(BF16)
        cs = _dot(tri2_ref[...], jnp.concatenate(_split2(nlw2), axis=0))
        e_pos = jnp.exp2(-cs)
        e_neg = jnp.exp2(cs)
        pack_ref[rs, 0:rw_w] = (-kk * jnp.exp2(nlw2 - cs)).astype(BF16)
        pack_ref[rs, rw_w:2 * rw_w] = (r * e_pos).astype(BF16)
        pack_ref[rs, 2 * rw_w:3 * rw_w] = (kk * a * e_neg).astype(BF16)
        pack_ref[rs, 3 * rw_w:4 * rw_w] = (k2 * e_neg).astype(BF16)
        for c in range(sub // CHUNK):
            n = s * (sub // CHUNK) + c
            elast_ref[n:n + 1, :] = e_pos[(c + 1) * CHUNK - 1:(c + 1) * CHUNK, :]
    prev_ref[...] = prev


def _row_block(n_rows, target):
    blk = min(target, n_rows)
    while n_rows % blk:
        blk //= 2
    return blk


def _vmem_limit(block_bytes):
    return int(min(V7X_VMEM_BYTES * 15 // 16, 2 * block_bytes + 16 * 1024 * 1024))


def _layer_spec(a, idx):
    nd = a.ndim
    return pl.BlockSpec((None,) + a.shape[1:], lambda *_: (idx,) + (0,) * (nd - 1))


def _inproj(x2, norm_w, w_bf16, v_first, p, seq):
    n, d = x2.shape
    n_in = w_bf16[0].shape[-1]
    rw_w = p["w0"][0].shape[-1]
    rw_shift = 3 * rw_w + LANES
    tm = _row_block(seq, 512)
    assert tm % (SUBLANES * CHUNK) == 0 and seq % tm == 0
    has_vres = v_first is not None

    def tok_spec(width):
        return pl.BlockSpec((tm, width), lambda i: (i, 0))

    def full_spec(a):
        nd = a.ndim
        return pl.BlockSpec(a.shape, lambda i: (0,) * nd)

    tri2, _, seg256 = _mixer_constants(2)
    args, in_specs = [x2], [tok_spec(d)]
    for a, idx in (norm_w, w_bf16):
        args.append(a)
        in_specs.append(_layer_spec(a, idx))
    if has_vres:
        args.append(v_first)
        in_specs.append(tok_spec(rw_w))
    names = ["mu", "w0", "a0", "wcode"] + (["v0", "vdn", "vup"] if has_vres else []) + ["kk", "ka", "rk"]
    for a, idx in [p[k] for k in names]:
        args.append(a)
        in_specs.append(_layer_spec(a, idx))
    for a in (tri2, seg256):
        args.append(a)
        in_specs.append(full_spec(a))
    out_shape = [jax.ShapeDtypeStruct((n, 4 * rw_w), BF16), jax.ShapeDtypeStruct((n, rw_w), BF16),
                 jax.ShapeDtypeStruct((n, rw_w), F32), jax.ShapeDtypeStruct((n // CHUNK, rw_w), F32),
                 jax.ShapeDtypeStruct((n, n_in - rw_shift), F32)]
    out_specs = [tok_spec(4 * rw_w), tok_spec(rw_w), tok_spec(rw_w),
                 pl.BlockSpec((tm // CHUNK, rw_w), lambda i: (i, 0)), tok_spec(n_in - rw_shift)]
    if not has_vres:
        out_shape.append(jax.ShapeDtypeStruct((n, rw_w), F32))
        out_specs.append(tok_spec(rw_w))
    block_bytes = tm * d * 4 + d * n_in * 2 + tm * (n_in + 4 * rw_w) * 4
    body = functools.partial(_inproj_kernel, has_vres=has_vres, rw_w=rw_w, tiles_per_seq=seq // tm)
    return pl.pallas_call(
        body,
        out_shape=out_shape,
        grid=(n // tm,),
        in_specs=in_specs,
        out_specs=out_specs,
        scratch_shapes=[pltpu.VMEM((1, rw_shift), F32)],
        compiler_params=pltpu.CompilerParams(
            dimension_semantics=("arbitrary",), vmem_limit_bytes=_vmem_limit(block_bytes)),
        name="inproj_vres" if has_vres else "inproj",
    )(*args)


def _stack_heads(x, lane_head0):
    return jnp.concatenate([jnp.where(lane_head0, x, 0.0), jnp.where(lane_head0, 0.0, x)], axis=0)


def _mix_kernel(*refs, layer, final, n_chunks, nb, rw_w, hg_w, lru_w):
    it = iter(refs)
    pack_ref, v_ref, bonus_ref, elast_ref, u_ref, x_ref, wout_ref = (next(it) for _ in range(7))
    fw_ref = next(it) if final else None
    lnw_ref, lnb_ref, lbraw_ref, hgnw_ref = (next(it) for _ in range(4))
    convw_ref, convb_ref, wa_ref, ba_ref, wx_ref, bx_ref, lam_ref = (next(it) for _ in range(7))
    hgm_ref, seg256_ref = (next(it) for _ in range(2))
    xout_ref = next(it)
    out_ref, rwstate_ref, hgstate_ref, lrux_ref, lruh_ref = (next(it) for _ in range(5))

    n_pairs = rw_w // LANES
    rows_all = nb * CHUNK
    o_rwz = 0
    o_hgq = o_rwz + rw_w
    o_hgf = o_hgq + hg_w
    o_hgi = o_hgf + hg_w
    o_hgz = o_hgi + hg_w
    o_lrx = o_hgz + hg_w
    o_lrz = o_lrx + lru_w

    @pl.when(pl.program_id(0) == 0)
    def _reset():
        rwstate_ref[...] = jnp.zeros_like(rwstate_ref)
        hgstate_ref[...] = jnp.zeros_like(hgstate_ref)
        lrux_ref[...] = jnp.zeros_like(lrux_ref)
        lruh_ref[...] = jnp.zeros_like(lruh_ref)

    row = _iota((rows_all, 1), 0)
    tpos = row & (CHUNK - 1)
    lane_head0 = _iota((1, LANES), 1) < HEAD
    ti = _iota((CHUNK, LANES), 0)
    tj = _iota((CHUNK, LANES), 1) & (CHUNK - 1)
    strict_lower = tj < ti
    incl_lower = tj <= ti
    eye = jnp.where(tj == ti, 1.0, 0.0)
    same_group = {size: (ti >> (size.bit_length() - 1)) == (tj >> (size.bit_length() - 1)) for size in HG_LEVELS}
    bd_state = (_iota((LANES, LANES), 0) >> LOG2_HEAD) == (_iota((LANES, LANES), 1) >> LOG2_HEAD)
    pairs = [(b, p) for p in range(n_pairs) for b in range(nb)]

    seg256 = seg256_ref[...]

    def segsum(x):
        return _head_sums(x, seg256)

    def stacked(m):
        return _stack_heads(m, lane_head0).astype(BF16)

    def per_batch(ref):
        val = ref[0]
        for b in range(1, nb):
            val = jnp.where(row >= b * CHUNK, ref[b], val)
        return val

    def tokens(ref, rows, lo, hi):
        return jnp.concatenate([ref[b, rows, lo:hi] for b in range(nb)], axis=0)

    def store(rows, lo, val):
        for b in range(nb):
            out_ref[b, rows, lo:lo + val.shape[1]] = val[b * CHUNK:(b + 1) * CHUNK].astype(out_ref.dtype)

    def blk(x, b, p):
        return x[b * CHUNK:(b + 1) * CHUNK, p * LANES:(p + 1) * LANES]

    lbraw = lbraw_ref[...]
    lbe = jnp.exp(lbraw - jnp.max(lbraw, axis=0, keepdims=True))
    lbw = lbe / jnp.sum(lbe, axis=0, keepdims=True)
    lb = lbw[0:1, :]
    for j in range(1, layer + 1):
        lb = lb + lbw[j:j + 1, :]
    lb = lb - lbw[0:1, :]

    def rwkv_stages(chunks):
        insts = [(ci, b, p) for ci in range(len(chunks)) for (b, p) in pairs]
        lhs, rhs, vs, a_ak, lrk, avs, pw, tinv = {}, {}, {}, {}, {}, {}, {}, {}
        for ci, (_, rows) in enumerate(chunks):
            a_t = tokens(pack_ref, rows, 0, rw_w)
            r_t = tokens(pack_ref, rows, rw_w, 2 * rw_w)
            b_t = tokens(pack_ref, rows, 2 * rw_w, 3 * rw_w)
            k_t = tokens(pack_ref, rows, 3 * rw_w, 4 * rw_w)
            v = tokens(v_ref, rows, 0, rw_w)
            for b, p in pairs:
                i = (ci, b, p)
                lhs[i] = jnp.concatenate([blk(a_t, b, p), blk(r_t, b, p)], axis=0)
                rhs[i] = jnp.concatenate([_stack_heads(blk(b_t, b, p), lane_head0),
                                          _stack_heads(blk(k_t, b, p), lane_head0)], axis=0)
                vs[i] = _stack_heads(blk(v, b, p), lane_head0)
        yield
        for i in insts:
            sc = _dot_nt(lhs[i], rhs[i])
            pw[i] = jnp.where(strict_lower, sc[0:CHUNK, 0:LANES], 0.0)
            a_ak[i] = jnp.concatenate([jnp.where(strict_lower, sc[0:CHUNK, LANES:2 * LANES], 0.0),
                                       jnp.where(incl_lower, sc[CHUNK:2 * CHUNK, LANES:2 * LANES], 0.0)],
                                      axis=0).astype(BF16)
            lrk[i] = jnp.where(incl_lower, sc[CHUNK:2 * CHUNK, 0:LANES], 0.0).astype(BF16)
        yield
        for i in insts:
            avs[i] = _dot(a_ak[i], vs[i])
            tinv[i] = eye + pw[i]
            pw[i] = _dot(pw[i].astype(BF16), stacked(pw[i]))
        yield
        for s in range(1, 5):
            for i in insts:
                both = _dot(jnp.concatenate([pw[i], tinv[i]], axis=0).astype(BF16), stacked(pw[i]))
                pw[i] = both[0:CHUNK]
                tinv[i] = tinv[i] + both[CHUNK:2 * CHUNK]
            yield
        for i in insts:
            tinv[i] = (tinv[i] + _dot(tinv[i].astype(BF16), stacked(pw[i]))).astype(BF16)
        yield
        ys = []
        for ci, (c, rows) in enumerate(chunks):
            st, am, uv, yb = {}, {}, {}, {}
            for n, (b, p) in enumerate(pairs):
                st[b, p] = rwstate_ref[n]
                am[b, p] = _dot_nt(lhs[ci, b, p], st[b, p].astype(BF16))
            yield
            for b, p in pairs:
                i = (ci, b, p)
                x = am[b, p][0:CHUNK] + avs[i][0:CHUNK]
                uv[b, p] = jnp.concatenate([stacked(_dot(tinv[i], stacked(x))), vs[i]], axis=0)
            yield
            for b, p in pairs:
                yb[b, p] = (am[b, p][CHUNK:2 * CHUNK] + avs[ci, b, p][CHUNK:2 * CHUNK]
                            + _dot(lrk[ci, b, p], uv[b, p][0:2 * CHUNK]))
            yield
            for n, (b, p) in enumerate(pairs):
                rwstate_ref[n] = ((st[b, p] + _dot_tn(uv[b, p], rhs[ci, b, p]))
                                  * elast_ref[b, c][:, p * LANES:(p + 1) * LANES])
            yield
            ys.append(jnp.concatenate(
                [jnp.concatenate([yb[b, p] for p in range(n_pairs)], axis=1) for b in range(nb)], axis=0))
        y = jnp.concatenate(ys, axis=0)
        mean = segsum(y) * (1.0 / HEAD)
        yc = y - mean
        yield
        var = segsum(yc * yc) * (1.0 / HEAD)
        yn = yc * lax.rsqrt(var + RW_GN_EPS) * lnw_ref[...] + lnb_ref[...]
        for ci, (_, rows) in enumerate(chunks):
            store(rows, 0, (yn[ci * rows_all:(ci + 1) * rows_all] + tokens(bonus_ref, rows, 0, rw_w))
                  * _silu(tokens(u_ref, rows, o_rwz, o_rwz + rw_w)))
            yield

    def hgrn_stages(rows):
        q = tokens(u_ref, rows, o_hgq, o_hgq + hg_w)
        f = lb + (1.0 - lb) * _sigmoid(tokens(u_ref, rows, o_hgf, o_hgf + hg_w))
        g = jnp.log2(f)
        kx = 1.0 - f
        iv = tokens(u_ref, rows, o_hgi, o_hgi + hg_w)
        bb = _dot(hgm_ref[...], jnp.concatenate(_split2(g), axis=0))
        part = lambda n: bb[n * rows_all:(n + 1) * rows_all]
        bsub = part(0)
        n_lv = len(HG_LEVELS)
        yield
        bchunk = part(n_lv + 1)
        qc = (q * jnp.exp2(bchunk)).astype(BF16)
        kc = (kx * jnp.exp2(part(n_lv + 2))).astype(BF16)
        ivb = iv.astype(BF16)
        ocs = {}
        for n, (b, p) in enumerate(pairs):
            hst = hgstate_ref[n]
            ocs[b, p] = _dot_nt(blk(qc, b, p), hst.astype(BF16))
            upd = jnp.where(bd_state, _dot_tn(blk(ivb, b, p), blk(kc, b, p)), 0.0)
            last = (b + 1) * CHUNK - 1
            hgstate_ref[n] = hst * jnp.exp2(bchunk[last:last + 1, p * LANES:(p + 1) * LANES]) + upd
        yield
        scs = {i: None for i in pairs}
        for lv, size in enumerate(HG_LEVELS):
            pst = jnp.exp2(part(1 + lv))
            upper = (row & (size - 1)) >= size // 2
            qs = jnp.where(upper, q * pst, 0.0)
            ks = jnp.where(upper, 0.0, kx * pst)
            for i in pairs:
                sc = jnp.where(same_group[size], _dot_nt(blk(qs, *i).astype(BF16), stacked(blk(ks, *i))), 0.0)
                scs[i] = sc if scs[i] is None else scs[i] + sc
            yield
        ostr = {i: _mm(scs[i], stacked(blk(iv, *i))) + ocs[i] for i in pairs}
        yield
        sub = (rows_all // SUBLANES, SUBLANES, hg_w)
        pos = _iota((1, SUBLANES, 1), 1) & (HG_SUB - 1)
        q3, b3, k3, v3 = (z.reshape(sub) for z in (q, bsub, kx, iv))
        wgts = [q * kx]
        for d in range(1, HG_SUB):
            wgt = jnp.where(pos >= d, q3 * jnp.exp2(b3 - pltpu.roll(b3, d, axis=1)) * pltpu.roll(k3, d, axis=1), 0.0)
            wgts.append(wgt.reshape(rows_all, hg_w))
        yield
        sums = segsum(jnp.concatenate(wgts, axis=0))
        yield
        acc = sums[0:rows_all] * iv
        for d in range(1, HG_SUB):
            acc = acc + sums[d * rows_all:(d + 1) * rows_all] * pltpu.roll(v3, d, axis=1).reshape(rows_all, hg_w)
        o = acc + jnp.concatenate(
            [jnp.concatenate([ostr[(b, p)] for p in range(n_pairs)], axis=1) for b in range(nb)], axis=0)
        ms = segsum(o * o) * (1.0 / HEAD)
        store(rows, rw_w, o * lax.rsqrt(ms + NORM_EPS) * hgnw_ref[...] * _silu(tokens(u_ref, rows, o_hgz, o_hgz + hg_w)))

    def lru_stages(rows):
        xb = tokens(u_ref, rows, o_lrx, o_lrx + lru_w)
        row8 = _iota((SUBLANES, 1), 0)
        yv = convb_ref[...] + convw_ref[CONV_WIDTH - 1:CONV_WIDTH, :] * xb
        for d in range(1, CONV_WIDTH):
            rolled = pltpu.roll(xb, d, axis=0)
            pieces = []
            for b in range(nb):
                tail = pltpu.roll(lrux_ref[b], d, axis=0)
                pieces.append(jnp.where(row8 < d, tail, rolled[b * CHUNK:b * CHUNK + SUBLANES]))
                pieces.append(rolled[b * CHUNK + SUBLANES:(b + 1) * CHUNK])
            yv = yv + convw_ref[CONV_WIDTH - 1 - d:CONV_WIDTH - d, :] * jnp.concatenate(pieces, axis=0)
        for b in range(nb):
            lrux_ref[b] = xb[(b + 1) * CHUNK - SUBLANES:(b + 1) * CHUNK]
        yield
        ybf = yv.astype(BF16)
        rg = _sigmoid(_dot(ybf, wa_ref[...]) + ba_ref[...])
        ig = _sigmoid(_dot(ybf, wx_ref[...]) + bx_ref[...])
        log_a = -LRU_C * rg * _softplus(-lam_ref[...])
        av = jnp.exp(log_a)
        th = jnp.tanh(log_a)
        gu = jnp.sqrt(-2.0 * th / (1.0 - th)) * (ig * yv)
        yield
        step = 1
        while step < CHUNK:
            keep = tpos >= step
            a_sh = jnp.where(keep, pltpu.roll(av, step, axis=0), 1.0)
            u_sh = jnp.where(keep, pltpu.roll(gu, step, axis=0), 0.0)
            gu = av * u_sh + gu
            av = av * a_sh
            step *= 2
            yield
        hv = gu + av * per_batch(lruh_ref)
        for b in range(nb):
            lruh_ref[b] = hv[(b + 1) * CHUNK - 1:(b + 1) * CHUNK]
        store(rows, rw_w + hg_w, hv * _silu(tokens(u_ref, rows, o_lrz, o_lrz + lru_w)))

    def project_rows(r0, n_rows):
        mixed = jnp.concatenate([out_ref[b, r0:r0 + n_rows, :] for b in range(nb)], axis=0)
        proj = _dot(mixed, wout_ref[...])
        yield
        for b in range(nb):
            xn = x_ref[b, r0:r0 + n_rows, :] + proj[b * n_rows:(b + 1) * n_rows]
            if final:
                ms = jnp.mean(xn * xn, axis=-1, keepdims=True)
                xn = xn * lax.rsqrt(ms + NORM_EPS) * fw_ref[...]
            xout_ref[b, r0:r0 + n_rows, :] = xn
        yield

    assert n_chunks % CHUNKS_PER_ITER == 0
    group_rows = CHUNKS_PER_ITER * CHUNK
    for gi in range(n_chunks // CHUNKS_PER_ITER):
        chunks = [(gi * CHUNKS_PER_ITER + j, pl.ds((gi * CHUNKS_PER_ITER + j) * CHUNK, CHUNK))
                  for j in range(CHUNKS_PER_ITER)]
        pending = ([rwkv_stages(chunks)] + [hgrn_stages(rows) for _, rows in chunks]
                   + [lru_stages(rows) for _, rows in chunks])
        if gi > 0:
            pending.append(project_rows((gi - 1) * group_rows, group_rows))
        while pending:
            for gen in list(pending):
                if next(gen, True):
                    pending.remove(gen)
    for _ in project_rows((n_chunks // CHUNKS_PER_ITER - 1) * group_rows, group_rows):
        pass


def _mixer_constants(nb):
    rows_all = nb * CHUNK
    t = np.arange(rows_all)
    same_chunk = (t[None, :] // CHUNK) == (t[:, None] // CHUNK)
    lower = t[None, :] <= t[:, None]
    tri2 = np.tile(lower & same_chunk, (1, 2))
    mats = [lower & ((t[None, :] // HG_SUB) == (t[:, None] // HG_SUB))]
    for size in HG_LEVELS:
        half = size // 2
        same = (t[None, :] // size) == (t[:, None] // size)
        upper_row = (t[:, None] % size) >= half
        upper_col = (t[None, :] % size) >= half
        mats.append(same & np.where(upper_row, upper_col & lower, (~upper_col) & (~lower)))
    mats.append(lower & same_chunk)
    mats.append((~lower) & same_chunk)
    hgm = np.tile(np.concatenate(mats, axis=0), (1, 2))
    h = np.arange(2 * LANES) // HEAD
    seg256 = h[None, :] == h[:, None]
    as_bf16 = lambda m: jnp.asarray(m.astype(np.float32), dtype=BF16)
    return as_bf16(tri2), as_bf16(hgm), as_bf16(seg256)


def _block_diag(w):
    nl, g, n, _ = w.shape
    out = jnp.zeros((nl, g * n, g * n), w.dtype)
    for i in range(g):
        out = out.at[:, i * n:(i + 1) * n, i * n:(i + 1) * n].set(w[:, i])
    return out


def _mixer(prep, x2, w_out_bf16, final_w, layer, p, batch, seq):
    pack, vb, bonus, elast, rest = prep
    d = x2.shape[1]
    rw_w = vb.shape[1]
    hg_w = p["hgnw"][0].shape[-1]
    lru_w = p["lam"][0].shape[-1]
    d_mix = rw_w + hg_w + lru_w
    tb = _row_block(seq, 512)
    assert tb % (CHUNKS_PER_ITER * CHUNK) == 0 and rw_w == 3 * LANES and hg_w == rw_w
    assert w_out_bf16[0].shape[-2] == d_mix

    def tok_spec(width):
        return pl.BlockSpec((batch, tb, width), lambda t: (0, t, 0))

    def full_spec(a):
        nd = a.ndim
        return pl.BlockSpec(a.shape, lambda t: (0,) * nd)

    per_token = lambda a: a.reshape(batch, seq, a.shape[1])
    args = [per_token(pack), per_token(vb), per_token(bonus), elast.reshape(batch, seq // CHUNK, 1, rw_w),
            per_token(rest), per_token(x2), w_out_bf16[0]]
    in_specs = [tok_spec(4 * rw_w), tok_spec(rw_w), tok_spec(rw_w),
                pl.BlockSpec((batch, tb // CHUNK, 1, rw_w), lambda t: (0, t, 0, 0)), tok_spec(rest.shape[1]),
                tok_spec(d), _layer_spec(*w_out_bf16)]
    if final_w is not None:
        args.append(final_w)
        in_specs.append(full_spec(final_w))
    names = ["lnw", "lnb", "lbraw", "hgnw", "convw", "convb", "wa", "ba", "wx", "bx", "lam"]
    for k in names:
        a, idx = p[k]
        args.append(a)
        in_specs.append(full_spec(a) if idx is None else _layer_spec(a, idx))
    _, hgm, seg256 = _mixer_constants(batch)
    for a in (hgm, seg256):
        args.append(a)
        in_specs.append(full_spec(a))

    n_pairs = rw_w // LANES
    scratch = [pltpu.VMEM((batch, tb, d_mix), BF16),
               pltpu.VMEM((batch * n_pairs, LANES, LANES), F32),
               pltpu.VMEM((batch * n_pairs, LANES, LANES), F32),
               pltpu.VMEM((batch, SUBLANES, lru_w), F32),
               pltpu.VMEM((batch, 1, lru_w), F32)]
    block_bytes = (batch * tb * (rest.shape[1] * 4 + 4 * rw_w * 2 + rw_w * 2 + rw_w * 4 + d_mix * 2 + 2 * d * 4)
                   + d_mix * d * 2)
    body = functools.partial(_mix_kernel, layer=layer, final=final_w is not None, n_chunks=tb // CHUNK, nb=batch,
                             rw_w=rw_w, hg_w=hg_w, lru_w=lru_w)
    x_new = pl.pallas_call(
        body,
        out_shape=jax.ShapeDtypeStruct((batch, seq, d), F32),
        grid=(seq // tb,),
        in_specs=in_specs,
        out_specs=tok_spec(d),
        scratch_shapes=scratch,
        compiler_params=pltpu.CompilerParams(
            dimension_semantics=("arbitrary",), vmem_limit_bytes=_vmem_limit(block_bytes)),
        name=f"mixer_l{layer}",
    )(*args)
    return x_new.reshape(batch * seq, d)


def kernel(x, norm_w, w_in, rw_mu, rw_w0, rw_w_up, rw_a0, rw_a_up, rw_v0, rw_v_dn, rw_v_up, rw_k_k, rw_k_a, rw_r_k, rw_ln_w, rw_ln_b, hg_lb_raw, hg_norm_w, lru_conv_w, lru_conv_b, lru_wa, lru_ba, lru_wx, lru_bx, lru_lambda, w_out, final_norm_w):
    batch, seq, d = x.shape
    depth = w_in.shape[0]
    rw_w = rw_w0.shape[1]
    lora = rw_w_up.shape[1]
    assert 2 * lora == LANES
    x2 = x.reshape(batch * seq, d)
    zeros_code = jnp.zeros((depth, lora, rw_w), F32)
    per_layer = lambda a: a.reshape(a.shape[0], 1, -1)
    stacked = {
        "norm": per_layer(norm_w), "win": w_in.astype(BF16), "wout": w_out.astype(BF16),
        "mu": per_layer(rw_mu), "w0": per_layer(rw_w0), "a0": per_layer(rw_a0),
        "wcode": jnp.concatenate([jnp.concatenate([rw_w_up, zeros_code], axis=2),
                                  jnp.concatenate([zeros_code, rw_a_up], axis=2)], axis=1).astype(BF16),
        "kk": per_layer(rw_k_k), "ka": per_layer(rw_k_a), "rk": per_layer(rw_r_k),
        "lnw": per_layer(rw_ln_w), "lnb": per_layer(rw_ln_b), "hgnw": per_layer(hg_norm_w),
        "convw": lru_conv_w, "convb": per_layer(lru_conv_b),
        "wa": _block_diag(lru_wa).astype(BF16), "ba": per_layer(lru_ba),
        "wx": _block_diag(lru_wx).astype(BF16), "bx": per_layer(lru_bx), "lam": per_layer(lru_lambda),
    }
    vres = {"v0": per_layer(rw_v0), "vdn": rw_v_dn.astype(BF16), "vup": rw_v_up.astype(BF16)}
    v_first = None
    for l in range(depth):
        p = {k: (a, l) for k, a in stacked.items()}
        p["lbraw"] = (hg_lb_raw, None)
        if l > 0:
            p.update({k: (a, l - 1) for k, a in vres.items()})
        prep = _inproj(x2, p["norm"], p["win"], v_first, p, seq)
        if l == 0:
            v_first = prep[5]
        fw = final_norm_w.reshape(1, -1) if l == depth - 1 else None
        x2 = _mixer(prep[:5], x2, p["wout"], fw, l, p, batch, seq)
    return x2.reshape(batch, seq, d)
```
